```python
import math
import jax
import jax.numpy as jnp
from jax import lax
import numpy as np

D_MODEL = 1024
BATCH = 4
SEQ = 4096
DEPTH = 2

GRID_W = 64
CTX_LEN = 256
N_MIXERS = 2
EPS = 1e-6
SHORT_CONV = 3

SSM_INNER = 2 * D_MODEL
SSM_HEAD_DIM = 64
SSM_HEADS = SSM_INNER // SSM_HEAD_DIM
SSM_GROUPS = 4
SSM_HPG = SSM_HEADS // SSM_GROUPS
SSM_STATE = 128
SSM_BC = SSM_GROUPS * SSM_STATE
SSM_CONV_CH = SSM_INNER + 2 * SSM_BC
SSM_IN = 2 * SSM_INNER + 2 * SSM_BC + 2 * SSM_HEADS
SSM_CHUNK = 128

MLSTM_HEADS = 4
MLSTM_QK_HEAD = D_MODEL // 2 // MLSTM_HEADS
MLSTM_V_HEAD = D_MODEL // MLSTM_HEADS
MLSTM_QK_WIDTH = MLSTM_HEADS * MLSTM_QK_HEAD
MLSTM_V_WIDTH = MLSTM_HEADS * MLSTM_V_HEAD
MLSTM_IN = 2 * MLSTM_QK_WIDTH + 2 * MLSTM_V_WIDTH + 4 * MLSTM_HEADS
MLSTM_CHUNK = 128
GATE_CAP = 15.0

D_FF = 128 * ((8 * D_MODEL // 3 + 127) // 128)
FFN_CONV = 3

kernel_name = "hybrid_ssd_mlstm_prefix_dit"


def rmsnorm(x, w):
    xf = x.astype(jnp.float32)
    xf = xf * lax.rsqrt(jnp.mean(xf * xf, axis=-1, keepdims=True) + EPS)
    return (xf * w.astype(jnp.float32)).astype(x.dtype)


def modulate(h, shift, scale):
    return h * (1 + scale) + shift


def dwconv1d(x, w, b):
    k = w.shape[0]
    y = lax.conv_general_dilated(x, w[:, None, :], window_strides=(1,), padding=[(k // 2, k // 2)],
                                 dimension_numbers=("NWC", "WIO", "NWC"), feature_group_count=x.shape[-1])
    return y + b


def dwconv2d(x, w, b):
    kh, kw = w.shape[:2]
    y = lax.conv_general_dilated(x, w[:, :, None, :], window_strides=(1, 1),
                                 padding=[(kh // 2, kh // 2), (kw // 2, kw // 2)],
                                 dimension_numbers=("NHWC", "HWIO", "NHWC"), feature_group_count=x.shape[-1])
    return y + b


def segsum(a):
    t = a.shape[-1]
    aa = jnp.broadcast_to(a[..., :, None], a.shape + (t,))
    strict = jnp.tril(jnp.ones((t, t), bool), -1)
    cs = jnp.cumsum(jnp.where(strict, aa, 0.0), axis=-2)
    return jnp.where(jnp.tril(jnp.ones((t, t), bool)), cs, -jnp.inf)


def ssd_chunked(xs, dt, a, bm, cm, init, with_output):
    f32 = jnp.float32
    bsz, l = xs.shape[:2]
    nc, q = l // SSM_CHUNK, SSM_CHUNK
    dt = dt.astype(f32)
    xd = (xs.astype(f32) * dt[..., None]).reshape(bsz, nc, q, SSM_GROUPS, SSM_HPG, SSM_HEAD_DIM)
    la = jnp.moveaxis((dt * a).reshape(bsz, nc, q, SSM_GROUPS, SSM_HPG), 2, -1)
    bc = bm.astype(f32).reshape(bsz, nc, q, SSM_GROUPS, SSM_STATE)
    la_cs = jnp.cumsum(la, axis=-1)
    decay_to_end = jnp.exp(la_cs[..., -1:] - la_cs)
    local = jnp.einsum("bclgn,bcgrl,bclgrp->bcgrpn", bc, decay_to_end, xd)
    states = jnp.concatenate([init.astype(f32)[:, None], local], axis=1)
    chunk_tot = jnp.pad(jnp.moveaxis(la_cs[..., -1], 1, -1), ((0, 0), (0, 0), (0, 0), (1, 0)))
    carried = jnp.einsum("bgrzc,bcgrpn->bzgrpn", jnp.exp(segsum(chunk_tot)), states)
    final = carried[:, -1]
    if not with_output:
        return None, final
    cc = cm.astype(f32).reshape(bsz, nc, q, SSM_GROUPS, SSM_STATE)
    cb = jnp.einsum("bclgn,bcsgn->bcgls", cc, bc)
    mix = cb[:, :, :, None] * jnp.exp(segsum(la))
    y = (jnp.einsum("bcgrls,bcsgrp->bclgrp", mix, xd)
         + jnp.einsum("bclgn,bcgrpn,bcgrl->bclgrp", cc, carried[:, :-1], jnp.exp(la_cs)))
    return y.reshape(bsz, l, SSM_GROUPS, SSM_HPG, SSM_HEAD_DIM), final


def ssd_direction(xs, dt, a, bm, cm, init, reverse, with_output):
    if reverse:
        xs, dt, bm, cm = (jnp.flip(t_, 1) for t_ in (xs, dt, bm, cm))
    y, final = ssd_chunked(xs, dt, a, bm, cm, init, with_output)
    if reverse and y is not None:
        y = jnp.flip(y, 1)
    return y, final


def mamba2_project(h, w_in, conv_w, conv_b):
    bsz, l, _ = h.shape
    z, xbc, dt = jnp.split(h @ w_in, [SSM_INNER, SSM_INNER + SSM_CONV_CH], axis=-1)
    xbc = jax.nn.silu(dwconv1d(xbc, conv_w, conv_b))
    xs, bm, cm = jnp.split(xbc, [SSM_INNER, SSM_INNER + SSM_BC], axis=-1)
    return (z,
            xs.reshape(bsz, l, SSM_GROUPS, SSM_HPG, SSM_HEAD_DIM),
            bm.reshape(bsz, l, SSM_GROUPS, SSM_STATE),
            cm.reshape(bsz, l, SSM_GROUPS, SSM_STATE),
            dt.reshape(bsz, l, 2, SSM_GROUPS, SSM_HPG))


def mamba2_output(y, xs, z, d, norm_w, w_out):
    bsz, l = z.shape[:2]
    y = (y.astype(z.dtype) + d * xs).reshape(bsz, l, SSM_INNER)
    return rmsnorm(y * jax.nn.silu(z), norm_w) @ w_out


def mamba2_mixer(hc, hl, w_in, conv_w, conv_b, dt_bias, a_log, d_skip, norm_w, w_out, ctx_out):
    zc, xc, bc, cc, dtc = mamba2_project(hc, w_in, conv_w, conv_b)
    zl, xl, bl, cl, dtl = mamba2_project(hl, w_in, conv_w, conv_b)
    a = -jnp.exp(a_log.astype(jnp.float32)).reshape(2, SSM_GROUPS, SSM_HPG)
    dtb = dt_bias.reshape(2, SSM_GROUPS, SSM_HPG)
    d = d_skip.reshape(SSM_GROUPS, SSM_HPG, 1)
    state0 = jnp.zeros((hl.shape[0], SSM_GROUPS, SSM_HPG, SSM_HEAD_DIM, SSM_STATE), jnp.float32)
    ys_l, ys_c = [], []
    for di in range(2):
        rev = di == 1
        y_c, s_c = ssd_direction(xc, jax.nn.softplus(dtc[:, :, di] + dtb[di]), a[di], bc, cc, state0, rev, ctx_out)
        y_l, _ = ssd_direction(xl, jax.nn.softplus(dtl[:, :, di] + dtb[di]), a[di], bl, cl, s_c, rev, True)
        ys_l.append(y_l)
        ys_c.append(y_c)
    out_l = mamba2_output(ys_l[0] + ys_l[1], xl, zl, d, norm_w, w_out)
    out_c = mamba2_output(ys_c[0] + ys_c[1], xc, zc, d, norm_w, w_out) if ctx_out else None
    return out_c, out_l


def mlstm_chunked(q, k, v, ig, lf, init, with_output):
    f32 = jnp.float32
    bsz, l = q.shape[:2]
    nc, t = l // MLSTM_CHUNK, MLSTM_CHUNK
    q = q.astype(f32).reshape(bsz, nc, t, MLSTM_HEADS, MLSTM_QK_HEAD)
    k = k.astype(f32).reshape(bsz, nc, t, MLSTM_HEADS, MLSTM_QK_HEAD)
    v = v.astype(f32).reshape(bsz, nc, t, MLSTM_HEADS, MLSTM_V_HEAD)
    ig = jnp.moveaxis(ig.astype(f32).reshape(bsz, nc, t, MLSTM_HEADS), 2, -1)
    fcum = jnp.cumsum(jnp.moveaxis(lf.astype(f32).reshape(bsz, nc, t, MLSTM_HEADS), 2, -1), axis=-1)
    f_tot = fcum[..., -1]
    w_end = f_tot[..., None] - fcum + ig
    m_loc = jnp.max(w_end, axis=-1)
    e = jnp.exp(w_end - m_loc[..., None])
    c_loc = jnp.einsum("bchs,bcshv,bcshk->bchvk", e, v, k)
    n_loc = jnp.einsum("bchs,bcshk->bchk", e, k)

    def step(carry, inp):
        c_prev, n_prev, m_prev = carry
        c_l, n_l, m_l, f_l = inp
        m_new = jnp.maximum(f_l + m_prev, m_l)
        a_prev = jnp.exp(f_l + m_prev - m_new)
        a_loc = jnp.exp(m_l - m_new)
        c_new = a_prev[..., None, None] * c_prev + a_loc[..., None, None] * c_l
        n_new = a_prev[..., None] * n_prev + a_loc[..., None] * n_l
        return (c_new, n_new, m_new), (c_prev, n_prev, m_prev)

    xs = tuple(jnp.moveaxis(t_, 1, 0) for t_ in (c_loc, n_loc, m_loc, f_tot))
    init = tuple(s_.astype(f32) for s_ in init)
    final, starts = lax.scan(step, init, xs)
    if not with_output:
        return None, final
    c0, n0, m0 = (jnp.moveaxis(s_, 0, 1) for s_ in starts)
    causal = jnp.tril(jnp.ones((t, t), bool))
    dlog = jnp.where(causal, fcum[..., :, None] - fcum[..., None, :] + ig[..., None, :], -jnp.inf)
    g = fcum + m0[..., None]
    m_comb = jnp.maximum(g, jnp.max(dlog, axis=-1))
    s = jnp.einsum("bcthk,bcshk->bchts", q, k) * jnp.exp(dlog - m_comb[..., None])
    inter = jnp.exp(g - m_comb)
    num = (jnp.einsum("bchts,bcshv->bcthv", s, v)
           + jnp.einsum("bcht,bchvk,bcthk->bcthv", inter, c0, q))
    den = jnp.sum(s, axis=-1) + inter * jnp.einsum("bchk,bcthk->bcht", n0, q)
    den = jnp.maximum(jnp.abs(den), jnp.exp(-m_comb))
    h = num / jnp.swapaxes(den, 2, 3)[..., None]
    return h.reshape(bsz, l, MLSTM_HEADS, MLSTM_V_HEAD), final


def mlstm_direction(q, k, v, ig, lf, init, reverse, with_output):
    if reverse:
        q, k, v, ig, lf = (jnp.flip(t_, 1) for t_ in (q, k, v, ig, lf))
    h, final = mlstm_chunked(q, k, v, ig, lf, init, with_output)
    if reverse and h is not None:
        h = jnp.flip(h, 1)
    return h, final


def mlstm_project(h, w_in, conv_w, conv_b, gate_b):
    bsz, l, _ = h.shape
    qk, v, o, gates = jnp.split(h @ w_in, [2 * MLSTM_QK_WIDTH, 2 * MLSTM_QK_WIDTH + MLSTM_V_WIDTH,
                                          2 * MLSTM_QK_WIDTH + 2 * MLSTM_V_WIDTH], axis=-1)
    q, k = jnp.split(jax.nn.silu(dwconv1d(qk, conv_w, conv_b)), 2, axis=-1)
    q = q.reshape(bsz, l, MLSTM_HEADS, MLSTM_QK_HEAD)
    k = k.reshape(bsz, l, MLSTM_HEADS, MLSTM_QK_HEAD) * (MLSTM_QK_HEAD ** -0.5)
    v = v.reshape(bsz, l, MLSTM_HEADS, MLSTM_V_HEAD)
    gates = (GATE_CAP * jnp.tanh((gates + gate_b).astype(jnp.float32) / GATE_CAP)).reshape(bsz, l, 4, MLSTM_HEADS)
    return q, k, v, jax.nn.sigmoid(o), gates[:, :, :2], jax.nn.log_sigmoid(gates[:, :, 2:])


def mlstm_output(h, o, norm_w, w_out):
    bsz, l = o.shape[:2]
    hn = h * lax.rsqrt(jnp.mean(h * h, axis=-1, keepdims=True) + EPS)
    y = (hn.reshape(bsz, l, MLSTM_V_WIDTH) * norm_w).astype(o.dtype) * o
    return y @ w_out


def mlstm_mixer(hc, hl, w_in, conv_w, conv_b, gate_b, norm_w, w_out, ctx_out):
    qc, kc, vc, oc, igc, lfc = mlstm_project(hc, w_in, conv_w, conv_b, gate_b)
    ql, kl, vl, ol, igl, lfl = mlstm_project(hl, w_in, conv_w, conv_b, gate_b)
    bsz = hl.shape[0]
    state0 = (jnp.zeros((bsz, MLSTM_HEADS, MLSTM_V_HEAD, MLSTM_QK_HEAD), jnp.float32),
              jnp.zeros((bsz, MLSTM_HEADS, MLSTM_QK_HEAD), jnp.float32),
              jnp.zeros((bsz, MLSTM_HEADS), jnp.float32))
    hs_l, hs_c = [], []
    for di in range(2):
        rev = di == 1
        h_c, s_c = mlstm_direction(qc, kc, vc, igc[:, :, di], lfc[:, :, di], state0, rev, ctx_out)
        h_l, _ = mlstm_direction(ql, kl, vl, igl[:, :, di], lfl[:, :, di], s_c, rev, True)
        hs_l.append(h_l)
        hs_c.append(h_c)
    out_l = mlstm_output(hs_l[0] + hs_l[1], ol, norm_w, w_out)
    out_c = mlstm_output(hs_c[0] + hs_c[1], oc, norm_w, w_out) if ctx_out else None
    return out_c, out_l


def conv_ffn(h, rows, w_up, conv_w, conv_b, w_down):
    bsz, l, _ = h.shape
    a, g = jnp.split(h @ w_up, 2, axis=-1)
    a = dwconv2d(a.reshape(bsz, rows, l // rows, D_FF), conv_w, conv_b).reshape(bsz, l, D_FF)
    return (jax.nn.silu(a) * g) @ w_down


def setup_inputs(seed: int = 0) -> dict:
    key = jax.random.key(seed)
    ks = iter(jax.random.split(key, 40))

    def nrm(shape, scale):
        return scale * jax.random.normal(next(ks), shape, jnp.float32)

    n_a = (DEPTH + N_MIXERS - 1) // N_MIXERS
    n_b = (DEPTH + N_MIXERS - 2) // N_MIXERS
    dt0 = jnp.exp(jax.random.uniform(next(ks), (n_a, 2, SSM_HEADS), jnp.float32,
                                     math.log(1e-3), math.log(1e-1)))
    ssm_dt_bias = dt0 + jnp.log(-jnp.expm1(-dt0))
    ssm_a_log = jnp.log(jax.random.uniform(next(ks), (n_a, 2, SSM_HEADS), jnp.float32, 1.0, 16.0))
    i_bias = nrm((n_b, 2, MLSTM_HEADS), 0.1)
    f_bias = jnp.linspace(3.0, 6.0, MLSTM_HEADS, dtype=jnp.float32) + nrm((n_b, 2, MLSTM_HEADS), 0.1)
    mlstm_gate_b = jnp.concatenate([i_bias, f_bias], axis=1).reshape(n_b, 4 * MLSTM_HEADS)
    return {
        "x": nrm((BATCH, SEQ, D_MODEL), 1.0),
        "c": nrm((BATCH, D_MODEL), 1.0),
        "ctx": nrm((BATCH, CTX_LEN, D_MODEL), 1.0),
        "c_ctx": nrm((D_MODEL,), 1.0),
        "ada_w": nrm((DEPTH, D_MODEL, 6 * D_MODEL), 0.5 * D_MODEL ** -0.5),
        "ada_b": nrm((DEPTH, 6 * D_MODEL), 0.02),
        "norm1_w": 1.0 + nrm((DEPTH, D_MODEL), 0.02),
        "norm2_w": 1.0 + nrm((DEPTH, D_MODEL), 0.02),
        "ssm_w_in": nrm((n_a, D_MODEL, SSM_IN), D_MODEL ** -0.5),
        "ssm_conv_w": nrm((n_a, SHORT_CONV, SSM_CONV_CH), SHORT_CONV ** -0.5),
        "ssm_conv_b": nrm((n_a, SSM_CONV_CH), 0.02),
        "ssm_dt_bias": ssm_dt_bias,
        "ssm_a_log": ssm_a_log,
        "ssm_d": 1.0 + nrm((n_a, SSM_HEADS), 0.1),
        "ssm_norm_w": 1.0 + nrm((n_a, SSM_INNER), 0.02),
        "ssm_w_out": nrm((n_a, SSM_INNER, D_MODEL), SSM_INNER ** -0.5),
        "mlstm_w_in": nrm((n_b, D_MODEL, MLSTM_IN), D_MODEL ** -0.5),
        "mlstm_conv_w": nrm((n_b, SHORT_CONV, 2 * MLSTM_QK_WIDTH), SHORT_CONV ** -0.5),
        "mlstm_conv_b": nrm((n_b, 2 * MLSTM_QK_WIDTH), 0.02),
        "mlstm_gate_b": mlstm_gate_b,
        "mlstm_norm_w": 1.0 + nrm((n_b, MLSTM_V_WIDTH), 0.02),
        "mlstm_w_out": nrm((n_b, MLSTM_V_WIDTH, D_MODEL), MLSTM_V_WIDTH ** -0.5),
        "ffn_w_up": nrm((DEPTH, D_MODEL, 2 * D_FF), D_MODEL ** -0.5),
        "ffn_conv_w": nrm((DEPTH, FFN_CONV, FFN_CONV, D_FF), 1.0 / FFN_CONV),
        "ffn_conv_b": nrm((DEPTH, D_FF), 0.02),
        "ffn_w_down": nrm((DEPTH, D_FF, D_MODEL), D_FF ** -0.5),
        "final_norm_w": 1.0 + nrm((D_MODEL,), 0.02),
    }


def reference(x, c, ctx, c_ctx, ada_w, ada_b, norm1_w, norm2_w,
              ssm_w_in, ssm_conv_w, ssm_conv_b, ssm_dt_bias, ssm_a_log, ssm_d, ssm_norm_w, ssm_w_out,
              mlstm_w_in, mlstm_conv_w, mlstm_conv_b, mlstm_gate_b, mlstm_norm_w, mlstm_w_out,
              ffn_w_up, ffn_conv_w, ffn_conv_b, ffn_w_down, final_norm_w):
    rows = x.shape[1] // GRID_W
    s_lat = jax.nn.silu(c)
    s_ctx = jax.nn.silu(c_ctx)
    for i in range(DEPTH):
        last = i == DEPTH - 1
        sh1, sc1, g1, sh2, sc2, g2 = (m_[:, None, :] for m_ in jnp.split(s_lat @ ada_w[i] + ada_b[i], 6, axis=-1))
        csh1, csc1, cg1, csh2, csc2, cg2 = jnp.split(s_ctx @ ada_w[i] + ada_b[i], 6, axis=-1)
        hl = modulate(rmsnorm(x, norm1_w[i]), sh1, sc1)
        hc = modulate(rmsnorm(ctx, norm1_w[i]), csh1, csc1)
        j = i // N_MIXERS
        if i % N_MIXERS == 0:
            oc, ol = mamba2_mixer(hc, hl, ssm_w_in[j], ssm_conv_w[j], ssm_conv_b[j], ssm_dt_bias[j],
                                  ssm_a_log[j], ssm_d[j], ssm_norm_w[j], ssm_w_out[j], not last)
        else:
            oc, ol = mlstm_mixer(hc, hl, mlstm_w_in[j], mlstm_conv_w[j], mlstm_conv_b[j], mlstm_gate_b[j],
                                 mlstm_norm_w[j], mlstm_w_out[j], not last)
        x = x + g1 * ol
        x = x + g2 * conv_ffn(modulate(rmsnorm(x, norm2_w[i]), sh2, sc2), rows,
                              ffn_w_up[i], ffn_conv_w[i], ffn_conv_b[i], ffn_w_down[i])
        if not last:
            ctx = ctx + cg1 * oc
            ctx = ctx + cg2 * conv_ffn(modulate(rmsnorm(ctx, norm2_w[i]), csh2, csc2), 1,
                                       ffn_w_up[i], ffn_conv_w[i], ffn_conv_b[i], ffn_w_down[i])
    return rmsnorm(x, final_norm_w)
```

```python
import functools

import jax
import jax.numpy as jnp
from jax import lax
from jax.experimental import pallas as pl
from jax.experimental.pallas import tpu as pltpu

F32 = jnp.float32
BF16 = jnp.bfloat16
HIGHEST = lax.Precision.HIGHEST

NORM_EPS = 1e-6
GRID_WIDTH = 64
SSM_HEAD_DIM = 64
SSM_GROUPS = 4
SSM_STATE = 128
MLSTM_HEADS = 4
GATE_CAP = 15.0
CHUNK = 128

LANES = 128
SUBLANES = 8
TOKEN_TILE = 256
COL_CHUNK = 512
VMEM_LIMIT = 56 * 1024 * 1024


def _params(n_axes):
    return pltpu.CompilerParams(
        dimension_semantics=("arbitrary",) * n_axes, vmem_limit_bytes=VMEM_LIMIT)


def _col_chunks(start, width, step=COL_CHUNK):
    out, c = [], start
    while c < start + width:
        w = min(step, start + width - c)
        out.append((c, w))
        c += w
    return out


def _silu(v):
    return v * jax.nn.sigmoid(v)


def _softplus(v):
    return jnp.maximum(v, 0.0) + jnp.log1p(jnp.exp(-jnp.abs(v)))


def _rms_scale(v):
    return v * lax.rsqrt(jnp.mean(v * v, axis=-1, keepdims=True) + NORM_EPS)


def _dot(a, b):
    return jnp.dot(a, b, preferred_element_type=F32)


def _dot_nt(a, b):
    return lax.dot_general(a, b, (((1,), (1,)), ((), ())), preferred_element_type=F32)


def _ada_kernel(c_ref, w_ref, b_ref, o_ref):
    s = _silu(c_ref[...])
    o_ref[...] = jnp.dot(s, w_ref[...], precision=HIGHEST, preferred_element_type=F32) + b_ref[...]


def _ada(cvec, ada_w, ada_b):
    depth, d, n = ada_w.shape
    rows = cvec.shape[0]
    tn = 512
    return pl.pallas_call(
        _ada_kernel,
        grid=(depth, n // tn),
        in_specs=[
            pl.BlockSpec((rows, d), lambda l, j: (0, 0)),
            pl.BlockSpec((None, d, tn), lambda l, j: (l, 0, j)),
            pl.BlockSpec((None, 1, tn), lambda l, j: (l, 0, j)),
        ],
        out_specs=pl.BlockSpec((None, rows, tn), lambda l, j: (l, 0, j)),
        out_shape=jax.ShapeDtypeStruct((depth, rows, n), F32),
        compiler_params=_params(2),
        name="ada_mod",
    )(cvec, ada_w, ada_b.reshape(depth, 1, n))


def _conv3_silu(main, prev_row, next_row, w_ref, b_ref, c0, cw):
    tm = main.shape[0]
    rows = lax.broadcasted_iota(jnp.int32, main.shape, 0)
    up = jnp.where(rows == 0, prev_row, pltpu.roll(main, 1, 0))
    dn = jnp.where(rows == tm - 1, next_row, pltpu.roll(main, tm - 1, 0))
    y = (up * w_ref[0:1, c0:c0 + cw] + main * w_ref[1:2, c0:c0 + cw]
         + dn * w_ref[2:3, c0:c0 + cw] + b_ref[:, c0:c0 + cw])
    return _silu(y)


def _proj_kernel(*refs, splits, conv_idx, seg_first, seg_last, tile_off):
    if conv_idx is None:
        x_ref, nw_ref, sh_ref, sc_ref, w_ref = refs[:5]
        outs = refs[5:]
    else:
        x_ref, xp_ref, xn_ref, nw_ref, sh_ref, sc_ref, w_ref, cw_ref, cb_ref = refs[:9]
        outs = refs[9:]
    j = pl.program_id(1) + tile_off

    def prep(v):
        h = _rms_scale(v) * nw_ref[...]
        return (h * (1.0 + sc_ref[...]) + sh_ref[...]).astype(BF16)

    hb = prep(x_ref[...])
    if conv_idx is not None:
        hp = prep(xp_ref[...])
        hn = prep(xn_ref[...])
        is_first = functools.reduce(jnp.logical_or, [j == t for t in seg_first])
        is_last = functools.reduce(jnp.logical_or, [j == t for t in seg_last])
    for oi, (s0, sw) in enumerate(splits):
        for (c0, cw) in _col_chunks(s0, sw):
            r = _dot(hb, w_ref[:, c0:c0 + cw])
            if oi == conv_idx:
                rp = _dot(hp, w_ref[:, c0:c0 + cw])[SUBLANES - 1:SUBLANES, :]
                rn = _dot(hn, w_ref[:, c0:c0 + cw])[0:1, :]
                rp = jnp.where(is_first, 0.0, rp)
                rn = jnp.where(is_last, 0.0, rn)
                r = _conv3_silu(r, rp, rn, cw_ref, cb_ref, c0 - s0, cw)
            outs[oi][:, c0 - s0:c0 - s0 + cw] = r


def _mod_spec(d, layer, k, row_fn):
    return pl.BlockSpec((None, None, None, 1, d), lambda b, j: (layer, row_fn(b, j), k, 0, 0))


def _project(xall, norm_w, mods, layer, k_shift, k_scale, w, splits, *, conv=None,
             n_ctx_tiles, ctx_row, tile_off=0, n_tiles=None):
    bsz, t, d = xall.shape
    tm = TOKEN_TILE
    tiles_total = t // tm
    if n_tiles is None:
        n_tiles = tiles_total - tile_off
    row_fn = lambda b, j: jnp.where(j + tile_off < n_ctx_tiles, ctx_row, b)
    tok = lambda c: pl.BlockSpec((None, tm, c), lambda b, j: (b, j + tile_off, 0))
    full = lambda a: pl.BlockSpec(a.shape, lambda b, j: (0,) * a.ndim)
    hb = tm // SUBLANES
    in_specs = [tok(d)]
    args = [xall]
    if conv is not None:
        conv_idx, conv_w, conv_b = conv
        in_specs += [
            pl.BlockSpec((None, SUBLANES, d),
                         lambda b, j: (b, jnp.maximum((j + tile_off) * hb - 1, 0), 0)),
            pl.BlockSpec((None, SUBLANES, d),
                         lambda b, j: (b, jnp.minimum((j + tile_off + 1) * hb, t // SUBLANES - 1), 0)),
        ]
        args += [xall, xall]
    else:
        conv_idx = None
    in_specs += [full(norm_w), _mod_spec(d, layer, k_shift, row_fn),
                 _mod_spec(d, layer, k_scale, row_fn),
                 pl.BlockSpec(w.shape, lambda b, j: (0, 0), pipeline_mode=pl.Buffered(1))]
    args += [norm_w, mods, mods, w]
    if conv is not None:
        in_specs += [full(conv_w), full(conv_b)]
        args += [conv_w, conv_b]
    out_specs = [pl.BlockSpec((None, tm, sw), lambda b, j: (b, j, 0)) for (_, sw) in splits]
    out_shape = [jax.ShapeDtypeStruct((bsz, n_tiles * tm, sw), F32) for (_, sw) in splits]
    kern = functools.partial(
        _proj_kernel, splits=tuple(splits), conv_idx=conv_idx,
        seg_first=(0, n_ctx_tiles), seg_last=(n_ctx_tiles - 1, tiles_total - 1), tile_off=tile_off)
    return pl.pallas_call(
        kern, grid=(bsz, n_tiles), in_specs=in_specs, out_specs=out_specs, out_shape=out_shape,
        compiler_params=_params(2), name=f"proj_l{layer}_k{k_shift}",
    )(*args)


def _ssd_kernel(xbc_ref, dt_ref, dtb_ref, alog_ref, y_ref, st_ref, *, reverse, heads):
    q = CHUNK
    hd, ng, ns = SSM_HEAD_DIM, SSM_GROUPS, SSM_STATE
    hpg = heads // ng
    inner = heads * hd
    dir_off = heads if reverse else 0

    @pl.when(pl.program_id(1) == 0)
    def _():
        st_ref[...] = jnp.zeros_like(st_ref)

    r_i = lax.broadcasted_iota(jnp.int32, (q, q), 0)
    c_i = lax.broadcasted_iota(jnp.int32, (q, q), 1)
    if reverse:
        tri = (c_i >= r_i)
        mask_sl = (r_i >= c_i)
    else:
        tri = (c_i <= r_i)
        mask_sl = (r_i <= c_i)

    dts_t = _softplus(dt_ref[...] + dtb_ref[...])
    la_t = dts_t * (-jnp.exp(alog_ref[...]))
    w_t = jnp.dot(tri.astype(F32), la_t, precision=HIGHEST, preferred_element_type=F32)
    w = w_t.T
    dts = dts_t.T
    tot = w[:, 0:1] if reverse else w[:, q - 1:q]
    dte = jnp.exp(tot - w)
    ew = jnp.exp(w)
    etot = jnp.exp(tot)

    for g in range(ng):
        bg = xbc_ref[:, inner + g * ns: inner + (g + 1) * ns]
        cg = xbc_ref[:, inner + ng * ns + g * ns: inner + ng * ns + (g + 1) * ns]
        bb = bg.astype(BF16)
        cb = cg.astype(BF16)
        cbt = _dot_nt(bb, cb)
        st_g = st_ref[g]
        yint = _dot_nt(st_g.astype(BF16), cb)
        y_parts, xdec_parts = [], []
        for pr in range(hpg // 2):
            col = (g * hpg + 2 * pr) * hd
            blk_t = xbc_ref[:, col:col + 2 * hd].T
            for half in range(2):
                r = 2 * pr + half
                hh = dir_off + g * hpg + r
                xd = blk_t[half * hd:(half + 1) * hd, :] * dts[hh:hh + 1, :]
                expo = jnp.where(mask_sl, w[hh:hh + 1, :] - w_t[:, hh:hh + 1], -jnp.inf)
                mt = (cbt * jnp.exp(expo)).astype(BF16)
                y_h = _dot(xd.astype(BF16), mt) + yint[r * hd:(r + 1) * hd, :] * ew[hh:hh + 1, :]
                y_parts.append(y_h)
                xdec_parts.append((xd * dte[hh:hh + 1, :]).astype(BF16))
        xdec = jnp.concatenate(xdec_parts, axis=0)
        sloc = _dot(xdec, bb)
        for r in range(hpg):
            hh = dir_off + g * hpg + r
            st_ref[g, r * hd:(r + 1) * hd, :] = (
                st_g[r * hd:(r + 1) * hd, :] * etot[hh:hh + 1, :] + sloc[r * hd:(r + 1) * hd, :])
        for pr in range(hpg // 2):
            y_blk = jnp.concatenate(y_parts[2 * pr:2 * pr + 2], axis=0)
            col = (g * hpg + 2 * pr) * hd
            y_ref[:, col:col + 2 * hd] = y_blk.T


def _chunk_order(n_ctx_chunks, n_chunks, reverse):
    if not reverse:
        return lambda i: i
    return lambda i: jnp.where(i < n_ctx_chunks, n_ctx_chunks - 1 - i, n_chunks + n_ctx_chunks - 1 - i)


def _ssd_scan(xbc, dt, dtb, alog, *, heads, reverse, n_ctx_chunks):
    bsz, t, cch = xbc.shape
    inner = heads * SSM_HEAD_DIM
    nc = t // CHUNK
    order = _chunk_order(n_ctx_chunks, nc, reverse)
    return pl.pallas_call(
        functools.partial(_ssd_kernel, reverse=reverse, heads=heads),
        grid=(bsz, nc),
        in_specs=[
            pl.BlockSpec((None, CHUNK, cch), lambda b, i: (b, order(i), 0)),
            pl.BlockSpec((None, CHUNK, LANES), lambda b, i: (b, order(i), 0)),
            pl.BlockSpec((1, LANES), lambda b, i: (0, 0)),
            pl.BlockSpec((1, LANES), lambda b, i: (0, 0)),
        ],
        out_specs=pl.BlockSpec((None, CHUNK, inner), lambda b, i: (b, order(i), 0)),
        out_shape=jax.ShapeDtypeStruct((bsz, t, inner), F32),
        scratch_shapes=[pltpu.VMEM((SSM_GROUPS, inner // SSM_GROUPS, SSM_STATE), F32)],
        compiler_params=_params(2),
        name="ssd_bwd" if reverse else "ssd_fwd",
    )(xbc, dt, dtb, alog)


def _ssm_out_kernel(yf_ref, yb_ref, xs_ref, z_ref, x_ref, g_ref, d_ref, nw_ref, w_ref, o_ref):
    z = z_ref[...]
    y = (yf_ref[...] + yb_ref[...] + d_ref[...] * xs_ref[...]) * _silu(z)
    yn = (_rms_scale(y) * nw_ref[...]).astype(BF16)
    o_ref[...] = x_ref[...] + g_ref[...] * _dot(yn, w_ref[...])


def _ssm_out(yf, yb, xbc, z, xall, mods, layer, d_row, norm_w, w_out, *, n_ctx_tiles, ctx_row):
    bsz, t, d = xall.shape
    inner = yf.shape[-1]
    tm = TOKEN_TILE
    row_fn = lambda b, j: jnp.where(j < n_ctx_tiles, ctx_row, b)
    tok = lambda c: pl.BlockSpec((None, tm, c), lambda b, j: (b, j, 0))
    full = lambda a: pl.BlockSpec(a.shape, lambda b, j: (0,) * a.ndim)
    return pl.pallas_call(
        _ssm_out_kernel, grid=(bsz, t // tm),
        in_specs=[tok(inner), tok(inner), tok(inner), tok(inner), tok(d),
                  _mod_spec(d, layer, 2, row_fn), full(d_row), full(norm_w), full(w_out)],
        out_specs=tok(d),
        out_shape=jax.ShapeDtypeStruct((bsz, t, d), F32),
        compiler_params=_params(2), name="ssm_out",
    )(yf, yb, xbc, z, xall, mods, d_row, norm_w, w_out)


def _mlstm_kernel(qk_ref, v_ref, gt_ref, gb_ref, h_ref, ct_ref, n_ref, m_ref, *, reverse):
    q = CHUNK
    nh = MLSTM_HEADS
    dk = qk_ref.shape[-1] // (2 * nh)
    dv = v_ref.shape[-1] // nh
    d = 1 if reverse else 0

    @pl.when(pl.program_id(1) == 0)
    def _():
        ct_ref[...] = jnp.zeros_like(ct_ref)
        n_ref[...] = jnp.zeros_like(n_ref)
        m_ref[...] = jnp.zeros_like(m_ref)

    r_i = lax.broadcasted_iota(jnp.int32, (q, q), 0)
    c_i = lax.broadcasted_iota(jnp.int32, (q, q), 1)
    keep = (c_i >= r_i) if reverse else (c_i <= r_i)

    gt = GATE_CAP * jnp.tanh((gt_ref[...] + gb_ref[...]) / GATE_CAP)
    lane_ok = c_i < nh
    ig_sh = (LANES - nh * d) % LANES
    fg_sh = LANES - (2 * nh + nh * d)
    ig_t = gt if ig_sh == 0 else pltpu.roll(gt, ig_sh, 1)
    ig_t = jnp.where(lane_ok, ig_t, 0.0)
    fr_t = pltpu.roll(gt, fg_sh, 1)
    lf_t = jnp.where(lane_ok, -_softplus(-fr_t), 0.0)
    wv_t = jnp.dot(keep.astype(F32), lf_t, precision=HIGHEST, preferred_element_type=F32)
    ftot = jnp.sum(lf_t, axis=0, keepdims=True)
    wend_t = ftot - wv_t + ig_t
    mloc = jnp.max(wend_t, axis=0, keepdims=True)
    e_t = jnp.exp(wend_t - mloc)
    m_prev = m_ref[...]
    m_new = jnp.maximum(ftot + m_prev, mloc)
    a_prev = jnp.exp(ftot + m_prev - m_new)
    a_loc = jnp.exp(mloc - m_new)
    gcol_t = wv_t + m_prev
    wv = wv_t.T
    e_r = e_t.T
    rowterm = ig_t.T - wv

    for h in range(nh):
        qh = qk_ref[:, h * dk:(h + 1) * dk]
        kh = qk_ref[:, nh * dk + h * dk: nh * dk + (h + 1) * dk] * (dk ** -0.5)
        vb = v_ref[:, h * dv:(h + 1) * dv].astype(BF16)
        qb = qh.astype(BF16)
        kt = kh.T
        dlog = jnp.where(keep, wv_t[:, h:h + 1] + rowterm[h:h + 1, :], -jnp.inf)
        mrow = jnp.max(dlog, axis=1, keepdims=True)
        gcol = gcol_t[:, h:h + 1]
        mcomb = jnp.maximum(gcol, mrow)
        s = _dot(qb, kt.astype(BF16)) * jnp.exp(dlog - mcomb)
        inter = jnp.exp(gcol - mcomb)
        ct_h = ct_ref[h]
        n_h = n_ref[h]
        num = _dot(s.astype(BF16), vb) + inter * _dot(qb, ct_h.astype(BF16))
        den = jnp.sum(s, axis=1, keepdims=True) + inter * jnp.sum(qh * n_h, axis=1, keepdims=True)
        den = jnp.maximum(jnp.abs(den), jnp.exp(-mcomb))
        h_ref[:, h * dv:(h + 1) * dv] = num / den
        ap = a_prev[:, h:h + 1]
        al = a_loc[:, h:h + 1]
        cloc = _dot((kt * e_r[h:h + 1, :]).astype(BF16), vb)
        ct_ref[h] = ap * ct_h + al * cloc
        nloc = jnp.sum(kh * e_t[:, h:h + 1], axis=0, keepdims=True)
        n_ref[h] = ap * n_h + al * nloc
    m_ref[...] = m_new


def _mlstm_scan(qk, v, gates, gate_b, *, reverse, n_ctx_chunks):
    bsz, t, qkw = qk.shape
    vw = v.shape[-1]
    nh = MLSTM_HEADS
    dk, dv = qkw // (2 * nh), vw // nh
    nc = t // CHUNK
    n_lat = nc - n_ctx_chunks
    order = _chunk_order(n_ctx_chunks, nc, reverse)
    first_lat = n_lat - 1 if reverse else 0
    out_idx = lambda i: jnp.where(i < n_ctx_chunks, first_lat, order(i) - n_ctx_chunks)
    return pl.pallas_call(
        functools.partial(_mlstm_kernel, reverse=reverse),
        grid=(bsz, nc),
        in_specs=[
            pl.BlockSpec((None, CHUNK, qkw), lambda b, i: (b, order(i), 0)),
            pl.BlockSpec((None, CHUNK, vw), lambda b, i: (b, order(i), 0)),
            pl.BlockSpec((None, CHUNK, LANES), lambda b, i: (b, order(i), 0)),
            pl.BlockSpec((1, LANES), lambda b, i: (0, 0)),
        ],
        out_specs=pl.BlockSpec((None, CHUNK, vw), lambda b, i: (b, out_idx(i), 0)),
        out_shape=jax.ShapeDtypeStruct((bsz, n_lat * CHUNK, vw), F32),
        scratch_shapes=[pltpu.VMEM((nh, dk, dv), F32), pltpu.VMEM((nh, 1, dk), F32),
                        pltpu.VMEM((1, LANES), F32)],
        compiler_params=_params(2),
        name="mlstm_bwd" if reverse else "mlstm_fwd",
    )(qk, v, gates, gate_b)


def _mlstm_out_kernel(hf_ref, hb_ref, o_ref_in, x_ref, g_ref, nw_ref, w_ref, out_ref):
    nh = MLSTM_HEADS
    dv = hf_ref.shape[-1] // nh
    parts = []
    for h in range(nh):
        hh = hf_ref[:, h * dv:(h + 1) * dv] + hb_ref[:, h * dv:(h + 1) * dv]
        parts.append(_rms_scale(hh))
    hn = jnp.concatenate(parts, axis=1)
    y = (hn * nw_ref[...] * jax.nn.sigmoid(o_ref_in[...])).astype(BF16)
    out_ref[...] = x_ref[...] + g_ref[...] * _dot(y, w_ref[...])


def _mlstm_out(hf, hb, o_gate, xall, mods, layer, norm_w, w_out, *, n_ctx_tiles):
    bsz, t_lat, vw = hf.shape
    d = xall.shape[-1]
    tm = TOKEN_TILE
    lat = lambda c: pl.BlockSpec((None, tm, c), lambda b, j: (b, j, 0))
    allt = lambda c: pl.BlockSpec((None, tm, c), lambda b, j: (b, j + n_ctx_tiles, 0))
    full = lambda a: pl.BlockSpec(a.shape, lambda b, j: (0,) * a.ndim)
    return pl.pallas_call(
        _mlstm_out_kernel, grid=(bsz, t_lat // tm),
        in_specs=[lat(vw), lat(vw), allt(vw), allt(d),
                  _mod_spec(d, layer, 2, lambda b, j: b), full(norm_w), full(w_out)],
        out_specs=lat(d),
        out_shape=jax.ShapeDtypeStruct((bsz, t_lat, d), F32),
        compiler_params=_params(2), name="mlstm_out",
    )(hf, hb, o_gate, xall, mods, norm_w, w_out)


def _ffn_down_kernel(*refs, n_ctx_tiles, tiles_total, tile_off, final_norm):
    if final_norm:
        a_ref, ap_ref, an_ref, gl_ref, x_ref, g_ref, cw_ref, cb_ref, w_ref, fw_ref, o_ref, ext_ref, acc_ref = refs
    else:
        a_ref, ap_ref, an_ref, gl_ref, x_ref, g_ref, cw_ref, cb_ref, w_ref, o_ref, ext_ref, acc_ref = refs
    tm, f = a_ref.shape
    gw = GRID_WIDTH
    j = pl.program_id(1) + tile_off
    is_ctx = j < n_ctx_tiles
    has_prev = jnp.logical_and(jnp.logical_not(is_ctx), j > n_ctx_tiles)
    has_next = jnp.logical_and(jnp.logical_not(is_ctx), j < tiles_total - 1)
    ext_ref[0:gw, :] = jnp.where(has_prev, ap_ref[...], 0.0)
    ext_ref[gw:gw + tm, :] = a_ref[...]
    ext_ref[gw + tm:gw + tm + gw, :] = jnp.where(has_next, an_ref[...], 0.0)

    for ci, (c0, cw) in enumerate(_col_chunks(0, f)):
        rows = lax.broadcasted_iota(jnp.int32, (tm, cw), 0)
        colpos = jnp.where(is_ctx, rows, rows & (gw - 1))
        last = jnp.where(is_ctx, tm - 1, gw - 1)
        left_ok = colpos != 0
        right_ok = colpos != last
        acc = jnp.zeros((tm, cw), F32) + cb_ref[:, c0:c0 + cw]
        for di in (-1, 0, 1):
            slab = ext_ref[gw + gw * di: gw + gw * di + tm, c0:c0 + cw]
            if di != 0:
                slab = jnp.where(is_ctx, 0.0, slab)
            for dj in (-1, 0, 1):
                if dj == 0:
                    tap = slab
                elif dj == -1:
                    tap = jnp.where(left_ok, pltpu.roll(slab, 1, 0), 0.0)
                else:
                    tap = jnp.where(right_ok, pltpu.roll(slab, tm - 1, 0), 0.0)
                k = 3 * (di + 1) + (dj + 1)
                acc = acc + tap * cw_ref[k:k + 1, c0:c0 + cw]
        hmid = (_silu(acc) * gl_ref[:, c0:c0 + cw]).astype(BF16)
        part = _dot(hmid, w_ref[c0:c0 + cw, :])
        if ci == 0:
            acc_ref[...] = part
        else:
            acc_ref[...] += part
    out = x_ref[...] + g_ref[...] * acc_ref[...]
    if final_norm:
        out = _rms_scale(out) * fw_ref[...]
    o_ref[...] = out


def _ffn_down(a, gl, xres, mods, layer, conv_w9, conv_b, w_down, *, n_ctx_tiles, tiles_total,
              tile_off, final_w=None):
    bsz, tt, f = a.shape
    d = xres.shape[-1]
    tm = TOKEN_TILE
    gw = GRID_WIDTH
    n_tiles = tt // tm
    hb = tm // gw
    row_fn = lambda b, j: jnp.where(j + tile_off < n_ctx_tiles, bsz, b)
    tok = lambda c: pl.BlockSpec((None, tm, c), lambda b, j: (b, j, 0))
    full = lambda arr: pl.BlockSpec(arr.shape, lambda b, j: (0,) * arr.ndim)
    in_specs = [
        tok(f),
        pl.BlockSpec((None, gw, f), lambda b, j: (b, jnp.maximum(j * hb - 1, 0), 0)),
        pl.BlockSpec((None, gw, f), lambda b, j: (b, jnp.minimum((j + 1) * hb, tt // gw - 1), 0)),
        tok(f), tok(d), _mod_spec(d, layer, 5, row_fn), full(conv_w9), full(conv_b), full(w_down)]
    args = [a, a, a, gl, xres, mods, conv_w9, conv_b, w_down]
    if final_w is not None:
        in_specs.append(full(final_w))
        args.append(final_w)
    kern = functools.partial(_ffn_down_kernel, n_ctx_tiles=n_ctx_tiles, tiles_total=tiles_total,
                             tile_off=tile_off, final_norm=final_w is not None)
    return pl.pallas_call(
        kern, grid=(bsz, n_tiles), in_specs=in_specs, out_specs=tok(d),
        out_shape=jax.ShapeDtypeStruct((bsz, tt, d), F32),
        scratch_shapes=[pltpu.VMEM((tm + 2 * gw, f), F32), pltpu.VMEM((tm, d), F32)],
        compiler_params=_params(2), name=f"ffn_down_l{layer}",
    )(*args)


def _pad_cols(a, n):
    return jnp.pad(a, [(0, 0)] * (a.ndim - 1) + [(0, n - a.shape[-1])])


def kernel(x, c, ctx, c_ctx, ada_w, ada_b, norm1_w, norm2_w, ssm_w_in, ssm_conv_w, ssm_conv_b, ssm_dt_bias, ssm_a_log, ssm_d, ssm_norm_w, ssm_w_out, mlstm_w_in, mlstm_conv_w, mlstm_conv_b, mlstm_gate_b, mlstm_norm_w, mlstm_w_out, ffn_w_up, ffn_conv_w, ffn_conv_b, ffn_w_down, final_norm_w):
    bsz, seq, d = x.shape
    n_ctx = ctx.shape[1]
    depth = ada_w.shape[0]
    assert depth == 2 and ssm_w_in.shape[0] == 1 and mlstm_w_in.shape[0] == 1
    assert seq % TOKEN_TILE == 0 and n_ctx % TOKEN_TILE == 0 and TOKEN_TILE % GRID_WIDTH == 0
    n_ctx_tiles = n_ctx // TOKEN_TILE
    n_ctx_chunks = n_ctx // CHUNK
    t = n_ctx + seq
    tiles_total = t // TOKEN_TILE
    ctx_row = bsz

    n_rows = -(-(bsz + 1) // SUBLANES) * SUBLANES
    cvec = jnp.zeros((n_rows, d), F32).at[:bsz].set(c).at[bsz].set(c_ctx)
    mods = _ada(cvec, ada_w, ada_b).reshape(depth, n_rows, 6, 1, d)

    xall = jnp.concatenate([ctx, x], axis=1)

    heads = ssm_d.shape[1]
    inner = heads * SSM_HEAD_DIM
    bc_w = 2 * SSM_GROUPS * SSM_STATE
    conv_ch = inner + bc_w
    w_in = _pad_cols(ssm_w_in[0], 2 * inner + bc_w + LANES).astype(BF16)
    splits = [(0, inner), (inner, conv_ch), (inner + conv_ch, LANES)]
    z, xbc, dt = _project(
        xall, norm1_w[0:1], mods, 0, 0, 1, w_in, splits,
        conv=(1, ssm_conv_w[0], ssm_conv_b[0:1]), n_ctx_tiles=n_ctx_tiles, ctx_row=ctx_row)
    dtb = _pad_cols(ssm_dt_bias[0].reshape(1, -1), LANES)
    alog = _pad_cols(ssm_a_log[0].reshape(1, -1), LANES)
    assert 2 * heads <= LANES
    y_b = _ssd_scan(xbc, dt, dtb, alog, heads=heads, reverse=True, n_ctx_chunks=n_ctx_chunks)
    y_f = _ssd_scan(xbc, dt, dtb, alog, heads=heads, reverse=False, n_ctx_chunks=n_ctx_chunks)
    d_row = jnp.repeat(ssm_d[0], SSM_HEAD_DIM).reshape(1, inner)
    xall = _ssm_out(y_f, y_b, xbc, z, xall, mods, 0, d_row, ssm_norm_w[0:1],
                    ssm_w_out[0].astype(BF16), n_ctx_tiles=n_ctx_tiles, ctx_row=ctx_row)

    f = ffn_conv_b.shape[1]
    a, gl = _project(xall, norm2_w[0:1], mods, 0, 3, 4, ffn_w_up[0].astype(BF16),
                     [(0, f), (f, f)], n_ctx_tiles=n_ctx_tiles, ctx_row=ctx_row)
    xall = _ffn_down(a, gl, xall, mods, 0, ffn_conv_w[0].reshape(9, f), ffn_conv_b[0:1],
                     ffn_w_down[0].astype(BF16), n_ctx_tiles=n_ctx_tiles, tiles_total=tiles_total,
                     tile_off=0)

    qkw = mlstm_conv_b.shape[1]
    vw = mlstm_norm_w.shape[1]
    w_in = _pad_cols(mlstm_w_in[0], qkw + 2 * vw + LANES).astype(BF16)
    splits = [(0, qkw), (qkw, vw), (qkw + vw, vw), (qkw + 2 * vw, LANES)]
    qk, v, o_gate, gates = _project(
        xall, norm1_w[1:2], mods, 1, 0, 1, w_in, splits,
        conv=(0, mlstm_conv_w[0], mlstm_conv_b[0:1]), n_ctx_tiles=n_ctx_tiles, ctx_row=ctx_row)
    gate_b = _pad_cols(mlstm_gate_b[0:1], LANES)
    h_b = _mlstm_scan(qk, v, gates, gate_b, reverse=True, n_ctx_chunks=n_ctx_chunks)
    h_f = _mlstm_scan(qk, v, gates, gate_b, reverse=False, n_ctx_chunks=n_ctx_chunks)
    xlat = _mlstm_out(h_f, h_b, o_gate, xall, mods, 1, mlstm_norm_w[0:1],
                      mlstm_w_out[0].astype(BF16), n_ctx_tiles=n_ctx_tiles)

    a, gl = _project(xlat, norm2_w[1:2], mods, 1, 3, 4, ffn_w_up[1].astype(BF16),
                     [(0, f), (f, f)], n_ctx_tiles=0, ctx_row=ctx_row)
    return _ffn_down(a, gl, xlat, mods, 1, ffn_conv_w[1].reshape(9, f), ffn_conv_b[1:2],
                     ffn_w_down[1].astype(BF16), n_ctx_tiles=0, tiles_total=seq // TOKEN_TILE,
                     tile_off=0, final_w=final_norm_w.reshape(1, d))
```

```python
import functools

import jax
import jax.numpy as jnp
from jax import lax
from jax.experimental import pallas as pl
from jax.experimental.pallas import tpu as pltpu

F32 = jnp.float32
BF16 = jnp.bfloat16
HIGHEST = lax.Precision.HIGHEST

NORM_EPS = 1e-6
GRID_WIDTH = 64
SSM_HEAD_DIM = 64
SSM_GROUPS = 4
SSM_STATE = 128
MLSTM_HEADS = 4
GATE_CAP = 15.0
CHUNK = 128

LANES = 128
SUBLANES = 8
TOKEN_TILE = 256
COL_CHUNK = 512
FFN_COL_CHUNK = 256
FFN_K_PIECE = 256
VMEM_LIMIT = 56 * 1024 * 1024


def _params(n_axes):
    return pltpu.CompilerParams(
        dimension_semantics=("arbitrary",) * n_axes, vmem_limit_bytes=VMEM_LIMIT)


def _col_chunks(start, width, step=COL_CHUNK):
    out, c = [], start
    while c < start + width:
        w = min(step, start + width - c)
        out.append((c, w))
        c += w
    return out


def _silu(v):
    return v * jax.nn.sigmoid(v)


def _softplus(v):
    return jnp.maximum(v, 0.0) + jnp.log1p(jnp.exp(-jnp.abs(v)))


def _rms_scale(v):
    return v * lax.rsqrt(jnp.mean(v * v, axis=-1, keepdims=True) + NORM_EPS)


def _dot(a, b):
    return jnp.dot(a, b, preferred_element_type=F32)


def _dot_nt(a, b):
    return lax.dot_general(a, b, (((1,), (1,)), ((), ())), preferred_element_type=F32)


def _full(a):
    return pl.BlockSpec(a.shape, lambda b, j: (0,) * a.ndim)


def _layer_spec(a, layer):
    return pl.BlockSpec((None,) + a.shape[1:], lambda b, j: (layer,) + (0,) * (a.ndim - 1),
                        pipeline_mode=pl.Buffered(1))


def _mod_spec(d, layer, k, row_fn):
    return pl.BlockSpec((None, None, None, 1, d), lambda b, j: (layer, row_fn(b, j), k, 0, 0))


def _ada_kernel(c_ref, w_ref, b_ref, o_ref):
    s = _silu(c_ref[...])
    o_ref[...] = jnp.dot(s, w_ref[...], precision=HIGHEST, preferred_element_type=F32) + b_ref[...]


def _ada(cvec, ada_w, ada_b):
    depth, d, n = ada_w.shape
    rows = cvec.shape[0]
    tn = 512
    return pl.pallas_call(
        _ada_kernel,
        grid=(depth, n // tn),
        in_specs=[
            pl.BlockSpec((rows, d), lambda l, j: (0, 0)),
            pl.BlockSpec((None, d, tn), lambda l, j: (l, 0, j)),
            pl.BlockSpec((None, 1, tn), lambda l, j: (l, 0, j)),
        ],
        out_specs=pl.BlockSpec((None, rows, tn), lambda l, j: (l, 0, j)),
        out_shape=jax.ShapeDtypeStruct((depth, rows, n), F32),
        compiler_params=_params(2),
        name="ada_mod",
    )(cvec, ada_w, ada_b.reshape(depth, 1, n))


def _tok_specs(arrs, rows, n_ctx_tiles, shift=0, clamp_hi=None):
    per = TOKEN_TILE // rows
    specs = []
    if len(arrs) == 1:
        (a,) = arrs
        hi = a.shape[1] // rows - 1
        specs.append(pl.BlockSpec(
            (None, rows, a.shape[2]), lambda b, j: (b, jnp.clip(j * per + shift, 0, hi), 0)))
    else:
        ctx, lat = arrs
        hi_c = ctx.shape[1] // rows - 1
        hi_l = lat.shape[1] // rows - 1
        specs.append(pl.BlockSpec(
            (None, rows, ctx.shape[2]), lambda b, j: (b, jnp.clip(j * per + shift, 0, hi_c), 0)))
        specs.append(pl.BlockSpec(
            (None, rows, lat.shape[2]),
            lambda b, j: (b, jnp.clip((j - n_ctx_tiles) * per + shift, 0, hi_l), 0)))
    return specs


def _tok_read(refs, is_ctx):
    if len(refs) == 1:
        return refs[0][...]
    return jnp.where(is_ctx, refs[0][...], refs[1][...])


def _proj_kernel(*refs, n_src, splits, conv_idx, n_ctx_tiles, tiles_total):
    refs = list(refs)
    x_refs = [refs.pop(0) for _ in range(n_src)]
    if conv_idx is not None:
        xp_refs = [refs.pop(0) for _ in range(n_src)]
        xn_refs = [refs.pop(0) for _ in range(n_src)]
    nw_ref, sh_ref, sc_ref, w_ref = refs[:4]
    refs = refs[4:]
    if conv_idx is not None:
        cw_ref, cb_ref = refs[:2]
        refs = refs[2:]
    outs = refs
    j = pl.program_id(1)
    is_ctx = j < n_ctx_tiles
    tm = x_refs[0].shape[0]

    def prep(v):
        h = _rms_scale(v) * nw_ref[...]
        return h * (1.0 + sc_ref[...]) + sh_ref[...]

    h_main = prep(_tok_read(x_refs, is_ctx))
    hb = h_main.astype(BF16)
    if conv_idx is not None:
        is_first = jnp.logical_or(j == 0, j == n_ctx_tiles)
        is_last = jnp.logical_or(j == n_ctx_tiles - 1, j == tiles_total - 1)
        hp = jnp.where(is_first, 0.0, prep(_tok_read(xp_refs, is_ctx)))
        hn = jnp.where(is_last, 0.0, prep(_tok_read(xn_refs, is_ctx)))
        hext = jnp.concatenate([hp, h_main, hn], axis=0).astype(BF16)
        te = tm + 2 * SUBLANES
    for oi, (s0, sw) in enumerate(splits):
        for (c0, cw) in _col_chunks(s0, sw):
            if oi == conv_idx:
                r = _dot(hext, w_ref[:, c0:c0 + cw])
                k0 = c0 - s0
                y = (pltpu.roll(r, 1, 0) * cw_ref[0:1, k0:k0 + cw]
                     + r * cw_ref[1:2, k0:k0 + cw]
                     + pltpu.roll(r, te - 1, 0) * cw_ref[2:3, k0:k0 + cw]
                     + cb_ref[:, k0:k0 + cw])
                r = _silu(y[SUBLANES:SUBLANES + tm, :])
            else:
                r = _dot(hb, w_ref[:, c0:c0 + cw])
            outs[oi][:, c0 - s0:c0 - s0 + cw] = r.astype(outs[oi].dtype)


def _project(srcs, norm_w, mods, layer, k_shift, k_scale, w, splits, out_dtypes, *, conv=None,
             n_ctx_tiles, ctx_row, name):
    bsz = srcs[0].shape[0]
    d = srcs[0].shape[2]
    t = sum(a.shape[1] for a in srcs)
    tm = TOKEN_TILE
    tiles_total = t // tm
    row_fn = lambda b, j: jnp.where(j < n_ctx_tiles, ctx_row, b)
    in_specs = _tok_specs(srcs, tm, n_ctx_tiles)
    args = list(srcs)
    conv_idx = None
    if conv is not None:
        conv_idx, conv_w, conv_b = conv
        per = tm // SUBLANES
        in_specs += _tok_specs(srcs, SUBLANES, n_ctx_tiles, shift=-1)
        in_specs += _tok_specs(srcs, SUBLANES, n_ctx_tiles, shift=per)
        args += list(srcs) * 2
    in_specs += [_full(norm_w), _mod_spec(d, layer, k_shift, row_fn),
                 _mod_spec(d, layer, k_scale, row_fn), _layer_spec(w, layer if w.shape[0] > 1 else 0)]
    args += [norm_w, mods, mods, w]
    if conv is not None:
        in_specs += [_full(conv_w), _full(conv_b)]
        args += [conv_w, conv_b]
    out_specs = [pl.BlockSpec((None, tm, sw), lambda b, j: (b, j, 0)) for (_, sw) in splits]
    out_shape = [jax.ShapeDtypeStruct((bsz, t, sw), dt) for (_, sw), dt in zip(splits, out_dtypes)]
    kern = functools.partial(
        _proj_kernel, n_src=len(srcs), splits=tuple(splits), conv_idx=conv_idx,
        n_ctx_tiles=n_ctx_tiles, tiles_total=tiles_total)
    return pl.pallas_call(
        kern, grid=(bsz, tiles_total), in_specs=in_specs, out_specs=out_specs, out_shape=out_shape,
        compiler_params=_params(2), name=name,
    )(*args)


def _ssd_prep(dt_ref, dtb_ref, alog_ref, *, reverse):
    q = CHUNK
    r_i = lax.broadcasted_iota(jnp.int32, (q, q), 0)
    c_i = lax.broadcasted_iota(jnp.int32, (q, q), 1)
    if reverse:
        tri = (c_i >= r_i)
        mask_sl = (r_i >= c_i)
    else:
        tri = (c_i <= r_i)
        mask_sl = (r_i <= c_i)

    dts_t = _softplus(dt_ref[...] + dtb_ref[...])
    la_t = dts_t * (-jnp.exp(alog_ref[...]))
    w_t = jnp.dot(tri.astype(F32), la_t, precision=HIGHEST, preferred_element_type=F32)
    w = w_t.T
    dts = dts_t.T
    tot = w[:, 0:1] if reverse else w[:, q - 1:q]
    dte = jnp.exp(tot - w)
    ew = jnp.exp(w)
    etot = jnp.exp(tot)
    return dict(mask_sl=mask_sl, w=w, w_t=w_t, dts=dts, dte=dte, ew=ew, etot=etot)


def _ssd_group(p, xbc_ref, st_ref, g, *, reverse, heads):
    hd, ng, ns = SSM_HEAD_DIM, SSM_GROUPS, SSM_STATE
    hpg = heads // ng
    inner = heads * hd
    dir_off = heads if reverse else 0
    w, w_t = p["w"], p["w_t"]
    bb = xbc_ref[:, inner + g * ns: inner + (g + 1) * ns]
    cb = xbc_ref[:, inner + ng * ns + g * ns: inner + ng * ns + (g + 1) * ns]
    cbt = _dot_nt(bb, cb)
    st_g = st_ref[g]
    yint = _dot_nt(st_g.astype(BF16), cb)
    y_parts, xdec_parts = [], []
    for pr in range(hpg // 2):
        col = (g * hpg + 2 * pr) * hd
        blk_t = xbc_ref[:, col:col + 2 * hd].astype(F32).T
        for half in range(2):
            r = 2 * pr + half
            hh = dir_off + g * hpg + r
            xd = blk_t[half * hd:(half + 1) * hd, :] * p["dts"][hh:hh + 1, :]
            expo = jnp.where(p["mask_sl"], w[hh:hh + 1, :] - w_t[:, hh:hh + 1], -jnp.inf)
            mt = (cbt * jnp.exp(expo)).astype(BF16)
            y_h = (_dot(xd.astype(BF16), mt)
                   + yint[r * hd:(r + 1) * hd, :] * p["ew"][hh:hh + 1, :])
            y_parts.append(y_h)
            xdec_parts.append((xd * p["dte"][hh:hh + 1, :]).astype(BF16))
    xdec = jnp.concatenate(xdec_parts, axis=0)
    sloc = _dot(xdec, bb)
    new_state = [st_g[r * hd:(r + 1) * hd, :] * p["etot"][dir_off + g * hpg + r:dir_off + g * hpg + r + 1, :]
                 + sloc[r * hd:(r + 1) * hd, :] for r in range(hpg)]
    y_blocks = [jnp.concatenate(y_parts[2 * pr:2 * pr + 2], axis=0).T for pr in range(hpg // 2)]
    return y_blocks, new_state


def _ssd_kernel(xf_ref, dtf_ref, xb_ref, dtb2_ref, dtb_ref, alog_ref, yf_ref, yb_ref,
                stf_ref, stb_ref, *, heads):
    @pl.when(pl.program_id(1) == 0)
    def _():
        stf_ref[...] = jnp.zeros_like(stf_ref)
        stb_ref[...] = jnp.zeros_like(stb_ref)

    hd = SSM_HEAD_DIM
    hpg = heads // SSM_GROUPS
    chains = [
        (_ssd_prep(dtf_ref, dtb_ref, alog_ref, reverse=False), xf_ref, stf_ref, yf_ref, False),
        (_ssd_prep(dtb2_ref, dtb_ref, alog_ref, reverse=True), xb_ref, stb_ref, yb_ref, True),
    ]
    for g in range(SSM_GROUPS):
        results = [_ssd_group(p, x_ref, st_ref, g, reverse=rev, heads=heads)
                   for (p, x_ref, st_ref, _, rev) in chains]
        for (_, _, st_ref, y_ref, _), (y_blocks, new_state) in zip(chains, results):
            for r, s_new in enumerate(new_state):
                st_ref[g, r * hd:(r + 1) * hd, :] = s_new
            for pr, blk in enumerate(y_blocks):
                col = (g * hpg + 2 * pr) * hd
                y_ref[:, col:col + 2 * hd] = blk.astype(y_ref.dtype)


def _chunk_order(n_ctx_chunks, n_chunks, reverse):
    if not reverse:
        return lambda i: i
    return lambda i: jnp.where(i < n_ctx_chunks, n_ctx_chunks - 1 - i, n_chunks + n_ctx_chunks - 1 - i)


def _ssd_scan(xbc, dt, dtb, alog, *, heads, n_ctx_chunks):
    bsz, t, cch = xbc.shape
    inner = heads * SSM_HEAD_DIM
    nc = t // CHUNK
    fwd = _chunk_order(n_ctx_chunks, nc, False)
    bwd = _chunk_order(n_ctx_chunks, nc, True)
    state = pltpu.VMEM((SSM_GROUPS, inner // SSM_GROUPS, SSM_STATE), F32)
    return pl.pallas_call(
        functools.partial(_ssd_kernel, heads=heads),
        grid=(bsz, nc),
        in_specs=[
            pl.BlockSpec((None, CHUNK, cch), lambda b, i: (b, fwd(i), 0)),
            pl.BlockSpec((None, CHUNK, LANES), lambda b, i: (b, fwd(i), 0)),
            pl.BlockSpec((None, CHUNK, cch), lambda b, i: (b, bwd(i), 0)),
            pl.BlockSpec((None, CHUNK, LANES), lambda b, i: (b, bwd(i), 0)),
            pl.BlockSpec((1, LANES), lambda b, i: (0, 0)),
            pl.BlockSpec((1, LANES), lambda b, i: (0, 0)),
        ],
        out_specs=[pl.BlockSpec((None, CHUNK, inner), lambda b, i: (b, fwd(i), 0)),
                   pl.BlockSpec((None, CHUNK, inner), lambda b, i: (b, bwd(i), 0))],
        out_shape=[jax.ShapeDtypeStruct((bsz, t, inner), BF16)] * 2,
        scratch_shapes=[state, state],
        compiler_params=_params(2),
        name="ssd_scan",
    )(xbc, dt, xbc, dt, dtb, alog)


def _ssm_out_kernel(*refs, n_ctx_tiles):
    yf_ref, yb_ref, xs_ref, z_ref, xc_ref, xl_ref, g_ref, d_ref, nw_ref, w_ref, o_ref = refs
    is_ctx = pl.program_id(1) < n_ctx_tiles
    z = z_ref[...].astype(F32)
    y = (yf_ref[...].astype(F32) + yb_ref[...].astype(F32)
         + d_ref[...] * xs_ref[...].astype(F32)) * _silu(z)
    yn = (_rms_scale(y) * nw_ref[...]).astype(BF16)
    o_ref[...] = _tok_read([xc_ref, xl_ref], is_ctx) + g_ref[...] * _dot(yn, w_ref[...])


def _ssm_out(yf, yb, xbc, z, srcs, mods, layer, d_row, norm_w, w_out, *, n_ctx_tiles, ctx_row):
    bsz, t, inner = yf.shape
    d = srcs[0].shape[2]
    tm = TOKEN_TILE
    row_fn = lambda b, j: jnp.where(j < n_ctx_tiles, ctx_row, b)
    tok = lambda c: pl.BlockSpec((None, tm, c), lambda b, j: (b, j, 0))
    return pl.pallas_call(
        functools.partial(_ssm_out_kernel, n_ctx_tiles=n_ctx_tiles), grid=(bsz, t // tm),
        in_specs=[tok(inner), tok(inner), tok(inner), tok(inner)]
        + _tok_specs(srcs, tm, n_ctx_tiles)
        + [_mod_spec(d, layer, 2, row_fn), _full(d_row), _full(norm_w), _layer_spec(w_out, 0)],
        out_specs=tok(d),
        out_shape=jax.ShapeDtypeStruct((bsz, t, d), F32),
        compiler_params=_params(2), name="ssm_out",
    )(yf, yb, xbc, z, *srcs, mods, d_row, norm_w, w_out)


def _mlstm_prep(gt_ref, gb_ref, m_ref, *, reverse):
    q = CHUNK
    nh = MLSTM_HEADS
    d = 1 if reverse else 0

    r_i = lax.broadcasted_iota(jnp.int32, (q, q), 0)
    c_i = lax.broadcasted_iota(jnp.int32, (q, q), 1)
    keep = (c_i >= r_i) if reverse else (c_i <= r_i)

    gt = GATE_CAP * jnp.tanh((gt_ref[...] + gb_ref[...]) / GATE_CAP)
    lane_ok = c_i < nh
    ig_sh = (LANES - nh * d) % LANES
    fg_sh = LANES - (2 * nh + nh * d)
    ig_t = gt if ig_sh == 0 else pltpu.roll(gt, ig_sh, 1)
    ig_t = jnp.where(lane_ok, ig_t, 0.0)
    fr_t = pltpu.roll(gt, fg_sh, 1)
    lf_t = jnp.where(lane_ok, -_softplus(-fr_t), 0.0)
    wv_t = jnp.dot(keep.astype(F32), lf_t, precision=HIGHEST, preferred_element_type=F32)
    ftot = jnp.sum(lf_t, axis=0, keepdims=True)
    wend_t = ftot - wv_t + ig_t
    mloc = jnp.max(wend_t, axis=0, keepdims=True)
    e_t = jnp.exp(wend_t - mloc)
    m_prev = m_ref[...]
    m_new = jnp.maximum(ftot + m_prev, mloc)
    a_prev = jnp.exp(ftot + m_prev - m_new)
    a_loc = jnp.exp(mloc - m_new)
    gcol_t = wv_t + m_prev
    wv = wv_t.T
    e_r = e_t.T
    rowterm = ig_t.T - wv
    return dict(keep=keep, wv_t=wv_t, rowterm=rowterm, gcol_t=gcol_t, e_t=e_t, e_r=e_r,
                a_prev=a_prev, a_loc=a_loc, m_new=m_new)


def _mlstm_head(p, qk_ref, v_ref, ct_ref, n_ref, h):
    nh = MLSTM_HEADS
    dk = qk_ref.shape[-1] // (2 * nh)
    dv = v_ref.shape[-1] // nh
    qb = qk_ref[:, h * dk:(h + 1) * dk]
    qh = qb.astype(F32)
    kh = qk_ref[:, nh * dk + h * dk: nh * dk + (h + 1) * dk].astype(F32) * (dk ** -0.5)
    vb = v_ref[:, h * dv:(h + 1) * dv]
    kt = kh.T
    dlog = jnp.where(p["keep"], p["wv_t"][:, h:h + 1] + p["rowterm"][h:h + 1, :], -jnp.inf)
    mrow = jnp.max(dlog, axis=1, keepdims=True)
    gcol = p["gcol_t"][:, h:h + 1]
    mcomb = jnp.maximum(gcol, mrow)
    s = _dot(qb, kt.astype(BF16)) * jnp.exp(dlog - mcomb)
    inter = jnp.exp(gcol - mcomb)
    ct_h = ct_ref[h]
    n_h = n_ref[h]
    num = _dot(s.astype(BF16), vb) + inter * _dot(qb, ct_h.astype(BF16))
    den = jnp.sum(s, axis=1, keepdims=True) + inter * jnp.sum(qh * n_h, axis=1, keepdims=True)
    den = jnp.maximum(jnp.abs(den), jnp.exp(-mcomb))
    ap = p["a_prev"][:, h:h + 1]
    al = p["a_loc"][:, h:h + 1]
    cloc = _dot((kt * p["e_r"][h:h + 1, :]).astype(BF16), vb)
    nloc = jnp.sum(kh * p["e_t"][:, h:h + 1], axis=0, keepdims=True)
    return num / den, ap * ct_h + al * cloc, ap * n_h + al * nloc


def _mlstm_kernel(qkf_ref, vf_ref, gtf_ref, qkb_ref, vb_ref, gtb_ref, gb_ref, hf_ref, hb_ref,
                  ctf_ref, nf_ref, mf_ref, ctb_ref, nb_ref, mb_ref):
    @pl.when(pl.program_id(1) == 0)
    def _():
        for r in (ctf_ref, nf_ref, mf_ref, ctb_ref, nb_ref, mb_ref):
            r[...] = jnp.zeros_like(r)

    dv = vf_ref.shape[-1] // MLSTM_HEADS
    for (gt_ref, qk_ref, v_ref, ct_ref, n_ref, m_ref, h_ref, rev) in (
            (gtf_ref, qkf_ref, vf_ref, ctf_ref, nf_ref, mf_ref, hf_ref, False),
            (gtb_ref, qkb_ref, vb_ref, ctb_ref, nb_ref, mb_ref, hb_ref, True)):
        p = _mlstm_prep(gt_ref, gb_ref, m_ref, reverse=rev)
        for h in range(MLSTM_HEADS):
            h_out, ct_new, n_new = _mlstm_head(p, qk_ref, v_ref, ct_ref, n_ref, h)
            h_ref[:, h * dv:(h + 1) * dv] = h_out.astype(h_ref.dtype)
            ct_ref[h] = ct_new
            n_ref[h] = n_new
        m_ref[...] = p["m_new"]


def _mlstm_scan(qk, v, gates, gate_b, *, n_ctx_chunks):
    bsz, t, qkw = qk.shape
    vw = v.shape[-1]
    nh = MLSTM_HEADS
    dk, dv = qkw // (2 * nh), vw // nh
    nc = t // CHUNK
    n_lat = nc - n_ctx_chunks
    fwd = _chunk_order(n_ctx_chunks, nc, False)
    bwd = _chunk_order(n_ctx_chunks, nc, True)
    out_f = lambda i: jnp.where(i < n_ctx_chunks, 0, fwd(i) - n_ctx_chunks)
    out_b = lambda i: jnp.where(i < n_ctx_chunks, n_lat - 1, bwd(i) - n_ctx_chunks)
    ins = lambda order: [
        pl.BlockSpec((None, CHUNK, qkw), lambda b, i: (b, order(i), 0)),
        pl.BlockSpec((None, CHUNK, vw), lambda b, i: (b, order(i), 0)),
        pl.BlockSpec((None, CHUNK, LANES), lambda b, i: (b, order(i), 0))]
    state = [pltpu.VMEM((nh, dk, dv), F32), pltpu.VMEM((nh, 1, dk), F32), pltpu.VMEM((1, LANES), F32)]
    return pl.pallas_call(
        _mlstm_kernel,
        grid=(bsz, nc),
        in_specs=ins(fwd) + ins(bwd) + [pl.BlockSpec((1, LANES), lambda b, i: (0, 0))],
        out_specs=[pl.BlockSpec((None, CHUNK, vw), lambda b, i: (b, out_f(i), 0)),
                   pl.BlockSpec((None, CHUNK, vw), lambda b, i: (b, out_b(i), 0))],
        out_shape=[jax.ShapeDtypeStruct((bsz, n_lat * CHUNK, vw), BF16)] * 2,
        scratch_shapes=state + state,
        compiler_params=_params(2),
        name="mlstm_scan",
    )(qk, v, gates, qk, v, gates, gate_b)


def _mlstm_out_kernel(hf_ref, hb_ref, o_ref_in, x_ref, g_ref, nw_ref, w_ref, out_ref):
    nh = MLSTM_HEADS
    dv = hf_ref.shape[-1] // nh
    parts = []
    for h in range(nh):
        hh = (hf_ref[:, h * dv:(h + 1) * dv].astype(F32) + hb_ref[:, h * dv:(h + 1) * dv].astype(F32))
        parts.append(_rms_scale(hh))
    hn = jnp.concatenate(parts, axis=1)
    y = (hn * nw_ref[...] * jax.nn.sigmoid(o_ref_in[...].astype(F32))).astype(BF16)
    out_ref[...] = x_ref[...] + g_ref[...] * _dot(y, w_ref[...])


def _mlstm_out(hf, hb, o_gate, xall, mods, layer, norm_w, w_out, *, n_ctx_tiles):
    bsz, t_lat, vw = hf.shape
    d = xall.shape[-1]
    tm = TOKEN_TILE
    lat = lambda c: pl.BlockSpec((None, tm, c), lambda b, j: (b, j, 0))
    allt = lambda c: pl.BlockSpec((None, tm, c), lambda b, j: (b, j + n_ctx_tiles, 0))
    return pl.pallas_call(
        _mlstm_out_kernel, grid=(bsz, t_lat // tm),
        in_specs=[lat(vw), lat(vw), allt(vw), allt(d),
                  _mod_spec(d, layer, 2, lambda b, j: b), _full(norm_w), _layer_spec(w_out, 0)],
        out_specs=lat(d),
        out_shape=jax.ShapeDtypeStruct((bsz, t_lat, d), F32),
        compiler_params=_params(2), name="mlstm_out",
    )(hf, hb, o_gate, xall, mods, norm_w, w_out)


def _ffn_down_kernel(*refs, n_ctx_tiles, tiles_total, final_norm):
    if final_norm:
        a_ref, ap_ref, an_ref, gl_ref, x_ref, g_ref, cw_ref, cb_ref, w_ref, fw_ref, o_ref = refs
    else:
        a_ref, ap_ref, an_ref, gl_ref, x_ref, g_ref, cw_ref, cb_ref, w_ref, o_ref = refs
    tm, f = a_ref.shape
    gw = GRID_WIDTH
    j = pl.program_id(1)
    is_ctx = j < n_ctx_tiles
    has_prev = jnp.logical_and(jnp.logical_not(is_ctx), j > n_ctx_tiles)
    has_next = jnp.logical_and(jnp.logical_not(is_ctx), j < tiles_total - 1)

    cw = FFN_COL_CHUNK
    vert = jnp.where(is_ctx, 0.0, 1.0)
    rows = lax.broadcasted_iota(jnp.int32, (tm, cw), 0)
    colpos = jnp.where(is_ctx, rows, rows & (gw - 1))
    left_ok = colpos != 0
    right_ok = colpos != jnp.where(is_ctx, tm - 1, gw - 1)
    acc_out = None
    for g0 in range(0, f, FFN_K_PIECE):
        g1 = min(g0 + FFN_K_PIECE, f)
        mids = []
        for c0 in range(g0, g1, cw):
            cs = slice(c0, c0 + cw)
            up = jnp.where(has_prev, ap_ref[:, cs], 0.0)
            dn = jnp.where(has_next, an_ref[:, cs], 0.0)
            slabs = [jnp.concatenate([up, a_ref[0:tm - gw, cs]], axis=0), a_ref[:, cs],
                     jnp.concatenate([a_ref[gw:tm, cs], dn], axis=0)]

            def vsum(dj):
                k = lambda di: 3 * (di + 1) + (dj + 1)
                return (slabs[0] * (cw_ref[k(-1):k(-1) + 1, cs] * vert)
                        + slabs[1] * cw_ref[k(0):k(0) + 1, cs]
                        + slabs[2] * (cw_ref[k(1):k(1) + 1, cs] * vert))

            acc = (vsum(0) + cb_ref[:, cs]
                   + jnp.where(left_ok, pltpu.roll(vsum(-1), 1, 0), 0.0)
                   + jnp.where(right_ok, pltpu.roll(vsum(1), tm - 1, 0), 0.0))
            mids.append((_silu(acc) * gl_ref[:, cs]).astype(BF16))
        part = _dot(jnp.concatenate(mids, axis=1), w_ref[g0:g1, :])
        acc_out = part if acc_out is None else acc_out + part
    out = x_ref[...] + g_ref[...] * acc_out
    if final_norm:
        out = _rms_scale(out) * fw_ref[...]
    o_ref[...] = out


def _ffn_down(a, gl, xres, mods, layer, conv_w9, conv_b, w_down, *, n_ctx_tiles, ctx_row,
              final_w=None):
    bsz, tt, f = a.shape
    d = xres.shape[-1]
    tm = TOKEN_TILE
    gw = GRID_WIDTH
    assert f % FFN_COL_CHUNK == 0
    n_tiles = tt // tm
    hb = tm // gw
    row_fn = lambda b, j: jnp.where(j < n_ctx_tiles, ctx_row, b)
    tok = lambda c: pl.BlockSpec((None, tm, c), lambda b, j: (b, j, 0))
    in_specs = [
        tok(f),
        pl.BlockSpec((None, gw, f), lambda b, j: (b, jnp.maximum(j * hb - 1, 0), 0)),
        pl.BlockSpec((None, gw, f), lambda b, j: (b, jnp.minimum((j + 1) * hb, tt // gw - 1), 0)),
        tok(f), tok(d), _mod_spec(d, layer, 5, row_fn), _full(conv_w9), _full(conv_b),
        _layer_spec(w_down, layer)]
    args = [a, a, a, gl, xres, mods, conv_w9, conv_b, w_down]
    if final_w is not None:
        in_specs.append(_full(final_w))
        args.append(final_w)
    kern = functools.partial(_ffn_down_kernel, n_ctx_tiles=n_ctx_tiles, tiles_total=n_tiles,
                             final_norm=final_w is not None)
    return pl.pallas_call(
        kern, grid=(bsz, n_tiles), in_specs=in_specs, out_specs=tok(d),
        out_shape=jax.ShapeDtypeStruct((bsz, tt, d), F32),
        compiler_params=_params(2), name=f"ffn_down_l{layer}",
    )(*args)


def _pad_cols(a, n):
    return jnp.pad(a, [(0, 0)] * (a.ndim - 1) + [(0, n - a.shape[-1])])


def kernel(x, c, ctx, c_ctx, ada_w, ada_b, norm1_w, norm2_w, ssm_w_in, ssm_conv_w, ssm_conv_b, ssm_dt_bias, ssm_a_log, ssm_d, ssm_norm_w, ssm_w_out, mlstm_w_in, mlstm_conv_w, mlstm_conv_b, mlstm_gate_b, mlstm_norm_w, mlstm_w_out, ffn_w_up, ffn_conv_w, ffn_conv_b, ffn_w_down, final_norm_w):
    bsz, seq, d = x.shape
    n_ctx = ctx.shape[1]
    depth = ada_w.shape[0]
    assert depth == 2 and ssm_w_in.shape[0] == 1 and mlstm_w_in.shape[0] == 1
    assert seq % TOKEN_TILE == 0 and n_ctx == TOKEN_TILE and TOKEN_TILE % GRID_WIDTH == 0
    n_ctx_tiles = n_ctx // TOKEN_TILE
    n_ctx_chunks = n_ctx // CHUNK
    ctx_row = bsz

    n_rows = -(-(bsz + 1) // SUBLANES) * SUBLANES
    cvec = jnp.zeros((n_rows, d), F32).at[:bsz].set(c).at[bsz].set(c_ctx)
    mods = _ada(cvec, ada_w, ada_b).reshape(depth, n_rows, 6, 1, d)

    w_up = ffn_w_up.astype(BF16)
    w_down = ffn_w_down.astype(BF16)
    f = ffn_conv_b.shape[1]
    conv9 = ffn_conv_w.reshape(depth, 9, f)

    heads = ssm_d.shape[1]
    assert 2 * heads <= LANES
    inner = heads * SSM_HEAD_DIM
    bc_w = 2 * SSM_GROUPS * SSM_STATE
    conv_ch = inner + bc_w
    w_in = _pad_cols(ssm_w_in, 2 * inner + bc_w + LANES).astype(BF16)
    splits = [(0, inner), (inner, conv_ch), (inner + conv_ch, LANES)]
    z, xbc, dt = _project(
        (ctx, x), norm1_w[0:1], mods, 0, 0, 1, w_in, splits, (BF16, BF16, F32),
        conv=(1, ssm_conv_w[0], ssm_conv_b[0:1]), n_ctx_tiles=n_ctx_tiles, ctx_row=ctx_row,
        name="in_proj_l0")
    dtb = _pad_cols(ssm_dt_bias[0].reshape(1, -1), LANES)
    alog = _pad_cols(ssm_a_log[0].reshape(1, -1), LANES)
    y_f, y_b = _ssd_scan(xbc, dt, dtb, alog, heads=heads, n_ctx_chunks=n_ctx_chunks)
    d_row = jnp.repeat(ssm_d[0], SSM_HEAD_DIM).reshape(1, inner)
    xall = _ssm_out(y_f, y_b, xbc, z, (ctx, x), mods, 0, d_row, ssm_norm_w[0:1],
                    ssm_w_out.astype(BF16), n_ctx_tiles=n_ctx_tiles, ctx_row=ctx_row)

    a, gl = _project((xall,), norm2_w[0:1], mods, 0, 3, 4, w_up, [(0, f), (f, f)], (F32, F32),
                     n_ctx_tiles=n_ctx_tiles, ctx_row=ctx_row, name="ffn_up_l0")
    xall = _ffn_down(a, gl, xall, mods, 0, conv9[0], ffn_conv_b[0:1], w_down,
                     n_ctx_tiles=n_ctx_tiles, ctx_row=ctx_row)

    qkw = mlstm_conv_b.shape[1]
    vw = mlstm_norm_w.shape[1]
    w_in = _pad_cols(mlstm_w_in, qkw + 2 * vw + LANES).astype(BF16)
    splits = [(0, qkw), (qkw, vw), (qkw + vw, vw), (qkw + 2 * vw, LANES)]
    qk, v, o_gate, gates = _project(
        (xall,), norm1_w[1:2], mods, 1, 0, 1, w_in, splits, (BF16, BF16, BF16, F32),
        conv=(0, mlstm_conv_w[0], mlstm_conv_b[0:1]), n_ctx_tiles=n_ctx_tiles, ctx_row=ctx_row,
        name="in_proj_l1")
    gate_b = _pad_cols(mlstm_gate_b[0:1], LANES)
    h_f, h_b = _mlstm_scan(qk, v, gates, gate_b, n_ctx_chunks=n_ctx_chunks)
    xlat = _mlstm_out(h_f, h_b, o_gate, xall, mods, 1, mlstm_norm_w[0:1],
                      mlstm_w_out.astype(BF16), n_ctx_tiles=n_ctx_tiles)

    a, gl = _project((xlat,), norm2_w[1:2], mods, 1, 3, 4, w_up, [(0, f), (f, f)], (F32, F32),
                     n_ctx_tiles=0, ctx_row=ctx_row, name="ffn_up_l1")
    return _ffn_down(a, gl, xlat, mods, 1, conv9[1], ffn_conv_b[1:2], w_down,
                     n_ctx_tiles=0, ctx_row=ctx_row, final_w=final_norm_w.reshape(1, d))
```

```python
import functools

import jax
import jax.numpy as jnp
from jax import lax
from jax.experimental import pallas as pl
from jax.experimental.pallas import tpu as pltpu

F32 = jnp.float32
BF16 = jnp.bfloat16
HIGHEST = lax.Precision.HIGHEST

NORM_EPS = 1e-6
GRID_WIDTH = 64
SSM_HEAD_DIM = 64
SSM_GROUPS = 4
SSM_STATE = 128
MLSTM_HEADS = 4
GATE_CAP = 15.0
CHUNK = 128

LANES = 128
SUBLANES = 8
TOKEN_TILE = 256
COL_CHUNK = 512
FFN_COL_CHUNK = 256
VMEM_LIMIT = 56 * 1024 * 1024


def _params(n_axes):
    return pltpu.CompilerParams(
        dimension_semantics=("arbitrary",) * n_axes, vmem_limit_bytes=VMEM_LIMIT)


def _col_chunks(start, width, step=COL_CHUNK):
    out, c = [], start
    while c < start + width:
        w = min(step, start + width - c)
        out.append((c, w))
        c += w
    return out


def _silu(v):
    return v * jax.nn.sigmoid(v)


def _softplus(v):
    return jnp.maximum(v, 0.0) + jnp.log1p(jnp.exp(-jnp.abs(v)))


def _rms_scale(v):
    return v * lax.rsqrt(jnp.mean(v * v, axis=-1, keepdims=True) + NORM_EPS)


def _dot(a, b):
    return jnp.dot(a, b, preferred_element_type=F32)


def _dot_nt(a, b):
    return lax.dot_general(a, b, (((1,), (1,)), ((), ())), preferred_element_type=F32)


def _full(a):
    return pl.BlockSpec(a.shape, lambda b, j: (0,) * a.ndim)


def _layer_spec(a, layer):
    return pl.BlockSpec((None,) + a.shape[1:], lambda b, j: (layer,) + (0,) * (a.ndim - 1),
                        pipeline_mode=pl.Buffered(1))


def _mod_spec(d, layer, k, row_fn):
    return pl.BlockSpec((None, None, None, 1, d), lambda b, j: (layer, row_fn(b, j), k, 0, 0))


def _ada_kernel(c_ref, w_ref, b_ref, o_ref):
    s = _silu(c_ref[...])
    o_ref[...] = jnp.dot(s, w_ref[...], precision=HIGHEST, preferred_element_type=F32) + b_ref[...]


def _ada(cvec, ada_w, ada_b):
    depth, d, n = ada_w.shape
    rows = cvec.shape[0]
    tn = 512
    return pl.pallas_call(
        _ada_kernel,
        grid=(depth, n // tn),
        in_specs=[
            pl.BlockSpec((rows, d), lambda l, j: (0, 0)),
            pl.BlockSpec((None, d, tn), lambda l, j: (l, 0, j)),
            pl.BlockSpec((None, 1, tn), lambda l, j: (l, 0, j)),
        ],
        out_specs=pl.BlockSpec((None, rows, tn), lambda l, j: (l, 0, j)),
        out_shape=jax.ShapeDtypeStruct((depth, rows, n), F32),
        compiler_params=_params(2),
        name="ada_mod",
    )(cvec, ada_w, ada_b.reshape(depth, 1, n))


def _tok_specs(arrs, rows, n_ctx_tiles, shift=0, clamp_hi=None):
    per = TOKEN_TILE // rows
    specs = []
    if len(arrs) == 1:
        (a,) = arrs
        hi = a.shape[1] // rows - 1
        specs.append(pl.BlockSpec(
            (None, rows, a.shape[2]), lambda b, j: (b, jnp.clip(j * per + shift, 0, hi), 0)))
    else:
        ctx, lat = arrs
        hi_c = ctx.shape[1] // rows - 1
        hi_l = lat.shape[1] // rows - 1
        specs.append(pl.BlockSpec(
            (None, rows, ctx.shape[2]), lambda b, j: (b, jnp.clip(j * per + shift, 0, hi_c), 0)))
        specs.append(pl.BlockSpec(
            (None, rows, lat.shape[2]),
            lambda b, j: (b, jnp.clip((j - n_ctx_tiles) * per + shift, 0, hi_l), 0)))
    return specs


def _tok_read(refs, is_ctx):
    if len(refs) == 1:
        return refs[0][...]
    return jnp.where(is_ctx, refs[0][...], refs[1][...])


def _proj_kernel(*refs, n_src, splits, conv_idx, n_ctx_tiles, tiles_total):
    refs = list(refs)
    x_refs = [refs.pop(0) for _ in range(n_src)]
    if conv_idx is not None:
        xp_refs = [refs.pop(0) for _ in range(n_src)]
        xn_refs = [refs.pop(0) for _ in range(n_src)]
    nw_ref, sh_ref, sc_ref, w_ref = refs[:4]
    refs = refs[4:]
    if conv_idx is not None:
        cw_ref, cb_ref = refs[:2]
        refs = refs[2:]
    outs = refs
    j = pl.program_id(1)
    is_ctx = j < n_ctx_tiles
    tm = x_refs[0].shape[0]

    def prep(v):
        h = _rms_scale(v) * nw_ref[...]
        return h * (1.0 + sc_ref[...]) + sh_ref[...]

    h_main = prep(_tok_read(x_refs, is_ctx))
    hb = h_main.astype(BF16)
    if conv_idx is not None:
        is_first = jnp.logical_or(j == 0, j == n_ctx_tiles)
        is_last = jnp.logical_or(j == n_ctx_tiles - 1, j == tiles_total - 1)
        hp = jnp.where(is_first, 0.0, prep(_tok_read(xp_refs, is_ctx)))
        hn = jnp.where(is_last, 0.0, prep(_tok_read(xn_refs, is_ctx)))
        hext = jnp.concatenate([hp, h_main, hn], axis=0).astype(BF16)
        te = tm + 2 * SUBLANES
    for oi, (s0, sw) in enumerate(splits):
        for (c0, cw) in _col_chunks(s0, sw):
            if oi == conv_idx:
                r = _dot(hext, w_ref[:, c0:c0 + cw])
                k0 = c0 - s0
                y = (pltpu.roll(r, 1, 0) * cw_ref[0:1, k0:k0 + cw]
                     + r * cw_ref[1:2, k0:k0 + cw]
                     + pltpu.roll(r, te - 1, 0) * cw_ref[2:3, k0:k0 + cw]
                     + cb_ref[:, k0:k0 + cw])
                r = _silu(y[SUBLANES:SUBLANES + tm, :])
            else:
                r = _dot(hb, w_ref[:, c0:c0 + cw])
            outs[oi][:, c0 - s0:c0 - s0 + cw] = r.astype(outs[oi].dtype)


def _project(srcs, norm_w, mods, layer, k_shift, k_scale, w, splits, out_dtypes, *, conv=None,
             n_ctx_tiles, ctx_row, name):
    bsz = srcs[0].shape[0]
    d = srcs[0].shape[2]
    t = sum(a.shape[1] for a in srcs)
    tm = TOKEN_TILE
    tiles_total = t // tm
    row_fn = lambda b, j: jnp.where(j < n_ctx_tiles, ctx_row, b)
    in_specs = _tok_specs(srcs, tm, n_ctx_tiles)
    args = list(srcs)
    conv_idx = None
    if conv is not None:
        conv_idx, conv_w, conv_b = conv
        per = tm // SUBLANES
        in_specs += _tok_specs(srcs, SUBLANES, n_ctx_tiles, shift=-1)
        in_specs += _tok_specs(srcs, SUBLANES, n_ctx_tiles, shift=per)
        args += list(srcs) * 2
    in_specs += [_full(norm_w), _mod_spec(d, layer, k_shift, row_fn),
                 _mod_spec(d, layer, k_scale, row_fn), _layer_spec(w, layer if w.shape[0] > 1 else 0)]
    args += [norm_w, mods, mods, w]
    if conv is not None:
        in_specs += [_full(conv_w), _full(conv_b)]
        args += [conv_w, conv_b]
    out_specs = [pl.BlockSpec((None, tm, sw), lambda b, j: (b, j, 0)) for (_, sw) in splits]
    out_shape = [jax.ShapeDtypeStruct((bsz, t, sw), dt) for (_, sw), dt in zip(splits, out_dtypes)]
    kern = functools.partial(
        _proj_kernel, n_src=len(srcs), splits=tuple(splits), conv_idx=conv_idx,
        n_ctx_tiles=n_ctx_tiles, tiles_total=tiles_total)
    return pl.pallas_call(
        kern, grid=(bsz, tiles_total), in_specs=in_specs, out_specs=out_specs, out_shape=out_shape,
        compiler_params=_params(2), name=name,
    )(*args)


def _ssd_prep(dt_ref, dtb_ref, alog_ref, *, reverse):
    q = CHUNK
    r_i = lax.broadcasted_iota(jnp.int32, (q, q), 0)
    c_i = lax.broadcasted_iota(jnp.int32, (q, q), 1)
    if reverse:
        tri = (c_i >= r_i)
        mask_sl = (r_i >= c_i)
    else:
        tri = (c_i <= r_i)
        mask_sl = (r_i <= c_i)

    dts_t = _softplus(dt_ref[...] + dtb_ref[...])
    la_t = dts_t * (-jnp.exp(alog_ref[...]))
    w_t = jnp.dot(tri.astype(F32), la_t, precision=HIGHEST, preferred_element_type=F32)
    w = w_t.T
    dts = dts_t.T
    tot = w[:, 0:1] if reverse else w[:, q - 1:q]
    dte = jnp.exp(tot - w)
    ew = jnp.exp(w)
    etot = jnp.exp(tot)
    return dict(mask_sl=mask_sl, w=w, w_t=w_t, dts=dts, dte=dte, ew=ew, etot=etot)


def _ssd_group(p, xbc_ref, st_ref, g, *, reverse, heads):
    hd, ng, ns = SSM_HEAD_DIM, SSM_GROUPS, SSM_STATE
    hpg = heads // ng
    inner = heads * hd
    dir_off = heads if reverse else 0
    w, w_t = p["w"], p["w_t"]
    bb = xbc_ref[:, inner + g * ns: inner + (g + 1) * ns]
    cb = xbc_ref[:, inner + ng * ns + g * ns: inner + ng * ns + (g + 1) * ns]
    cbt = _dot_nt(bb, cb)
    st_g = st_ref[g]
    yint = _dot_nt(st_g.astype(BF16), cb)
    y_parts, xdec_parts = [], []
    for pr in range(hpg // 2):
        col = (g * hpg + 2 * pr) * hd
        blk_t = xbc_ref[:, col:col + 2 * hd].astype(F32).T
        for half in range(2):
            r = 2 * pr + half
            hh = dir_off + g * hpg + r
            xd = blk_t[half * hd:(half + 1) * hd, :] * p["dts"][hh:hh + 1, :]
            expo = jnp.where(p["mask_sl"], w[hh:hh + 1, :] - w_t[:, hh:hh + 1], -jnp.inf)
            mt = (cbt * jnp.exp(expo)).astype(BF16)
            y_h = (_dot(xd.astype(BF16), mt)
                   + yint[r * hd:(r + 1) * hd, :] * p["ew"][hh:hh + 1, :])
            y_parts.append(y_h)
            xdec_parts.append((xd * p["dte"][hh:hh + 1, :]).astype(BF16))
    xdec = jnp.concatenate(xdec_parts, axis=0)
    sloc = _dot(xdec, bb)
    new_state = [st_g[r * hd:(r + 1) * hd, :] * p["etot"][dir_off + g * hpg + r:dir_off + g * hpg + r + 1, :]
                 + sloc[r * hd:(r + 1) * hd, :] for r in range(hpg)]
    y_blocks = [jnp.concatenate(y_parts[2 * pr:2 * pr + 2], axis=0).T for pr in range(hpg // 2)]
    return y_blocks, new_state


def _ssd_kernel(xf_ref, dtf_ref, xb_ref, dtb2_ref, dtb_ref, alog_ref, yf_ref, yb_ref,
                stf_ref, stb_ref, *, heads):
    @pl.when(pl.program_id(1) == 0)
    def _():
        stf_ref[...] = jnp.zeros_like(stf_ref)
        stb_ref[...] = jnp.zeros_like(stb_ref)

    hd = SSM_HEAD_DIM
    hpg = heads // SSM_GROUPS
    chains = [
        (_ssd_prep(dtf_ref, dtb_ref, alog_ref, reverse=False), xf_ref, stf_ref, yf_ref, False),
        (_ssd_prep(dtb2_ref, dtb_ref, alog_ref, reverse=True), xb_ref, stb_ref, yb_ref, True),
    ]
    for g in range(SSM_GROUPS):
        results = [_ssd_group(p, x_ref, st_ref, g, reverse=rev, heads=heads)
                   for (p, x_ref, st_ref, _, rev) in chains]
        for (_, _, st_ref, y_ref, _), (y_blocks, new_state) in zip(chains, results):
            for r, s_new in enumerate(new_state):
                st_ref[g, r * hd:(r + 1) * hd, :] = s_new
            for pr, blk in enumerate(y_blocks):
                col = (g * hpg + 2 * pr) * hd
                y_ref[:, col:col + 2 * hd] = blk.astype(y_ref.dtype)


def _chunk_order(n_ctx_chunks, n_chunks, reverse):
    if not reverse:
        return lambda i: i
    return lambda i: jnp.where(i < n_ctx_chunks, n_ctx_chunks - 1 - i, n_chunks + n_ctx_chunks - 1 - i)


def _ssd_scan(xbc, dt, dtb, alog, *, heads, n_ctx_chunks):
    bsz, t, cch = xbc.shape
    inner = heads * SSM_HEAD_DIM
    nc = t // CHUNK
    fwd = _chunk_order(n_ctx_chunks, nc, False)
    bwd = _chunk_order(n_ctx_chunks, nc, True)
    state = pltpu.VMEM((SSM_GROUPS, inner // SSM_GROUPS, SSM_STATE), F32)
    return pl.pallas_call(
        functools.partial(_ssd_kernel, heads=heads),
        grid=(bsz, nc),
        in_specs=[
            pl.BlockSpec((None, CHUNK, cch), lambda b, i: (b, fwd(i), 0)),
            pl.BlockSpec((None, CHUNK, LANES), lambda b, i: (b, fwd(i), 0)),
            pl.BlockSpec((None, CHUNK, cch), lambda b, i: (b, bwd(i), 0)),
            pl.BlockSpec((None, CHUNK, LANES), lambda b, i: (b, bwd(i), 0)),
            pl.BlockSpec((1, LANES), lambda b, i: (0, 0)),
            pl.BlockSpec((1, LANES), lambda b, i: (0, 0)),
        ],
        out_specs=[pl.BlockSpec((None, CHUNK, inner), lambda b, i: (b, fwd(i), 0)),
                   pl.BlockSpec((None, CHUNK, inner), lambda b, i: (b, bwd(i), 0))],
        out_shape=[jax.ShapeDtypeStruct((bsz, t, inner), BF16)] * 2,
        scratch_shapes=[state, state],
        compiler_params=_params(2),
        name="ssd_scan",
    )(xbc, dt, xbc, dt, dtb, alog)


def _ssm_out_kernel(*refs, n_ctx_tiles):
    yf_ref, yb_ref, xs_ref, z_ref, xc_ref, xl_ref, g_ref, d_ref, nw_ref, w_ref, o_ref = refs
    is_ctx = pl.program_id(1) < n_ctx_tiles
    z = z_ref[...].astype(F32)
    y = (yf_ref[...].astype(F32) + yb_ref[...].astype(F32)
         + d_ref[...] * xs_ref[...].astype(F32)) * _silu(z)
    yn = (_rms_scale(y) * nw_ref[...]).astype(BF16)
    o_ref[...] = _tok_read([xc_ref, xl_ref], is_ctx) + g_ref[...] * _dot(yn, w_ref[...])


def _ssm_out(yf, yb, xbc, z, srcs, mods, layer, d_row, norm_w, w_out, *, n_ctx_tiles, ctx_row):
    bsz, t, inner = yf.shape
    d = srcs[0].shape[2]
    tm = TOKEN_TILE
    row_fn = lambda b, j: jnp.where(j < n_ctx_tiles, ctx_row, b)
    tok = lambda c: pl.BlockSpec((None, tm, c), lambda b, j: (b, j, 0))
    return pl.pallas_call(
        functools.partial(_ssm_out_kernel, n_ctx_tiles=n_ctx_tiles), grid=(bsz, t // tm),
        in_specs=[tok(inner), tok(inner), tok(inner), tok(inner)]
        + _tok_specs(srcs, tm, n_ctx_tiles)
        + [_mod_spec(d, layer, 2, row_fn), _full(d_row), _full(norm_w), _layer_spec(w_out, 0)],
        out_specs=tok(d),
        out_shape=jax.ShapeDtypeStruct((bsz, t, d), F32),
        compiler_params=_params(2), name="ssm_out",
    )(yf, yb, xbc, z, *srcs, mods, d_row, norm_w, w_out)


def _mlstm_prep(gt_ref, gb_ref, m_ref, *, reverse):
    q = CHUNK
    nh = MLSTM_HEADS
    d = 1 if reverse else 0

    r_i = lax.broadcasted_iota(jnp.int32, (q, q), 0)
    c_i = lax.broadcasted_iota(jnp.int32, (q, q), 1)
    keep = (c_i >= r_i) if reverse else (c_i <= r_i)

    gt = GATE_CAP * jnp.tanh((gt_ref[...] + gb_ref[...]) / GATE_CAP)
    lane_ok = c_i < nh
    ig_sh = (LANES - nh * d) % LANES
    fg_sh = LANES - (2 * nh + nh * d)
    ig_t = gt if ig_sh == 0 else pltpu.roll(gt, ig_sh, 1)
    ig_t = jnp.where(lane_ok, ig_t, 0.0)
    fr_t = pltpu.roll(gt, fg_sh, 1)
    lf_t = jnp.where(lane_ok, -_softplus(-fr_t), 0.0)
    wv_t = jnp.dot(keep.astype(F32), lf_t, precision=HIGHEST, preferred_element_type=F32)
    ftot = jnp.sum(lf_t, axis=0, keepdims=True)
    wend_t = ftot - wv_t + ig_t
    mloc = jnp.max(wend_t, axis=0, keepdims=True)
    e_t = jnp.exp(wend_t - mloc)
    m_prev = m_ref[...]
    m_new = jnp.maximum(ftot + m_prev, mloc)
    a_prev = jnp.exp(ftot + m_prev - m_new)
    a_loc = jnp.exp(mloc - m_new)
    run = ig_t - wv_t
    sh = 1
    while sh < q:
        if reverse:
            moved, ok = pltpu.roll(run, q - sh, 0), r_i < q - sh
        else:
            moved, ok = pltpu.roll(run, sh, 0), r_i >= sh
        run = jnp.maximum(run, jnp.where(ok, moved, -jnp.inf))
        sh *= 2
    gcol_t = wv_t + m_prev
    mcomb_t = jnp.maximum(gcol_t, wv_t + run)
    wv = wv_t.T
    return dict(keep=keep, wm_t=wv_t - mcomb_t, rowterm=ig_t.T - wv, inter_t=jnp.exp(gcol_t - mcomb_t),
                floor_t=jnp.exp(-mcomb_t), e_r=e_t.T, a_prev=a_prev, a_loc=a_loc, m_new=m_new)


def _mlstm_head(p, qk_ref, v_ref, c_ref, h):
    q = CHUNK
    nh = MLSTM_HEADS
    dk = qk_ref.shape[-1] // (2 * nh)
    dv = v_ref.shape[-1] // nh
    qb = qk_ref[:, h * dk:(h + 1) * dk]
    kt = (qk_ref[:, nh * dk + h * dk: nh * dk + (h + 1) * dk].astype(F32) * (dk ** -0.5)).T
    vaug = jnp.concatenate([v_ref[:, h * dv:(h + 1) * dv], jnp.ones((q, LANES), BF16)], axis=1)
    caug = c_ref[h]
    lanes = lambda col: jnp.broadcast_to(col, (q, LANES))
    pm = jnp.exp(jnp.where(p["keep"], p["wm_t"][:, h:h + 1] + p["rowterm"][h:h + 1, :], -jnp.inf))
    inter = lanes(p["inter_t"][:, h:h + 1])
    floor = lanes(p["floor_t"][:, h:h + 1])
    s = (_dot(qb, kt.astype(BF16)) * pm).astype(BF16)
    sv = _dot(s, vaug)
    qc = _dot(qb, caug.astype(BF16))
    blocks = [sv[:, c:c + LANES] + inter * qc[:, c:c + LANES] for c in range(0, dv + LANES, LANES)]
    rden = 1.0 / jnp.maximum(jnp.abs(blocks[-1]), floor)
    h_out = jnp.concatenate([b * rden for b in blocks[:-1]], axis=1)
    ap = p["a_prev"][:, h:h + 1]
    al = p["a_loc"][:, h:h + 1]
    c_new = ap * caug + al * _dot((kt * p["e_r"][h:h + 1, :]).astype(BF16), vaug)
    return h_out, c_new


def _mlstm_kernel(qkf_ref, vf_ref, gtf_ref, qkb_ref, vb_ref, gtb_ref, gb_ref, hf_ref, hb_ref,
                  cf_ref, mf_ref, cb_ref, mb_ref):
    @pl.when(pl.program_id(1) == 0)
    def _():
        for r in (cf_ref, mf_ref, cb_ref, mb_ref):
            r[...] = jnp.zeros_like(r)

    dv = vf_ref.shape[-1] // MLSTM_HEADS
    for (gt_ref, qk_ref, v_ref, c_ref, m_ref, h_ref, rev) in (
            (gtf_ref, qkf_ref, vf_ref, cf_ref, mf_ref, hf_ref, False),
            (gtb_ref, qkb_ref, vb_ref, cb_ref, mb_ref, hb_ref, True)):
        p = _mlstm_prep(gt_ref, gb_ref, m_ref, reverse=rev)
        m_ref[...] = p["m_new"]
        for h in range(MLSTM_HEADS):
            h_out, c_new = _mlstm_head(p, qk_ref, v_ref, c_ref, h)
            h_ref[:, h * dv:(h + 1) * dv] = h_out.astype(h_ref.dtype)
            c_ref[h] = c_new


def _mlstm_scan(qk, v, gates, gate_b, *, n_ctx_chunks):
    bsz, t, qkw = qk.shape
    vw = v.shape[-1]
    nh = MLSTM_HEADS
    dk, dv = qkw // (2 * nh), vw // nh
    nc = t // CHUNK
    n_lat = nc - n_ctx_chunks
    fwd = _chunk_order(n_ctx_chunks, nc, False)
    bwd = _chunk_order(n_ctx_chunks, nc, True)
    out_f = lambda i: jnp.where(i < n_ctx_chunks, 0, fwd(i) - n_ctx_chunks)
    out_b = lambda i: jnp.where(i < n_ctx_chunks, n_lat - 1, bwd(i) - n_ctx_chunks)
    ins = lambda order: [
        pl.BlockSpec((None, CHUNK, qkw), lambda b, i: (b, order(i), 0)),
        pl.BlockSpec((None, CHUNK, vw), lambda b, i: (b, order(i), 0)),
        pl.BlockSpec((None, CHUNK, LANES), lambda b, i: (b, order(i), 0))]
    state = [pltpu.VMEM((nh, dk, dv + LANES), F32), pltpu.VMEM((1, LANES), F32)]
    return pl.pallas_call(
        _mlstm_kernel,
        grid=(bsz, nc),
        in_specs=ins(fwd) + ins(bwd) + [pl.BlockSpec((1, LANES), lambda b, i: (0, 0))],
        out_specs=[pl.BlockSpec((None, CHUNK, vw), lambda b, i: (b, out_f(i), 0)),
                   pl.BlockSpec((None, CHUNK, vw), lambda b, i: (b, out_b(i), 0))],
        out_shape=[jax.ShapeDtypeStruct((bsz, n_lat * CHUNK, vw), BF16)] * 2,
        scratch_shapes=state + state,
        compiler_params=_params(2),
        name="mlstm_scan",
    )(qk, v, gates, qk, v, gates, gate_b)


def _mlstm_out_kernel(hf_ref, hb_ref, o_ref_in, x_ref, g_ref, nw_ref, w_ref, out_ref):
    nh = MLSTM_HEADS
    dv = hf_ref.shape[-1] // nh
    parts = []
    for h in range(nh):
        hh = (hf_ref[:, h * dv:(h + 1) * dv].astype(F32) + hb_ref[:, h * dv:(h + 1) * dv].astype(F32))
        parts.append(_rms_scale(hh))
    hn = jnp.concatenate(parts, axis=1)
    y = (hn * nw_ref[...] * jax.nn.sigmoid(o_ref_in[...].astype(F32))).astype(BF16)
    out_ref[...] = x_ref[...] + g_ref[...] * _dot(y, w_ref[...])


def _mlstm_out(hf, hb, o_gate, xall, mods, layer, norm_w, w_out, *, n_ctx_tiles):
    bsz, t_lat, vw = hf.shape
    d = xall.shape[-1]
    tm = TOKEN_TILE
    lat = lambda c: pl.BlockSpec((None, tm, c), lambda b, j: (b, j, 0))
    allt = lambda c: pl.BlockSpec((None, tm, c), lambda b, j: (b, j + n_ctx_tiles, 0))
    return pl.pallas_call(
        _mlstm_out_kernel, grid=(bsz, t_lat // tm),
        in_specs=[lat(vw), lat(vw), allt(vw), allt(d),
                  _mod_spec(d, layer, 2, lambda b, j: b), _full(norm_w), _layer_spec(w_out, 0)],
        out_specs=lat(d),
        out_shape=jax.ShapeDtypeStruct((bsz, t_lat, d), F32),
        compiler_params=_params(2), name="mlstm_out",
    )(hf, hb, o_gate, xall, mods, norm_w, w_out)


def _ffn_down_kernel(*refs, n_ctx_tiles, tiles_total, final_norm):
    if final_norm:
        a_ref, ap_ref, an_ref, gl_ref, x_ref, g_ref, cw_ref, cb_ref, w_ref, fw_ref, o_ref = refs
    else:
        a_ref, ap_ref, an_ref, gl_ref, x_ref, g_ref, cw_ref, cb_ref, w_ref, o_ref = refs
    tm, f = a_ref.shape
    gw = GRID_WIDTH
    j = pl.program_id(1)
    is_ctx = j < n_ctx_tiles
    has_prev = jnp.logical_and(jnp.logical_not(is_ctx), j > n_ctx_tiles)
    has_next = jnp.logical_and(jnp.logical_not(is_ctx), j < tiles_total - 1)

    cw = FFN_COL_CHUNK
    vert = jnp.where(is_ctx, 0.0, 1.0)
    ri = lax.broadcasted_iota(jnp.int32, (tm, tm), 0)
    ci = lax.broadcasted_iota(jnp.int32, (tm, tm), 1)
    colpos = jnp.where(is_ctx, ri, ri & (gw - 1))
    last_col = jnp.where(is_ctx, tm - 1, gw - 1)
    take_left = jnp.where(colpos != 0, jnp.where(ci == ri - 1, 1.0, 0.0), 0.0)
    take_right = jnp.where(colpos != last_col, jnp.where(ci == ri + 1, 1.0, 0.0), 0.0)
    shift = jnp.concatenate([take_left, take_right], axis=1).astype(BF16)

    def taps(c0):
        cs = slice(c0, c0 + cw)
        up = jnp.where(has_prev, ap_ref[:, cs], 0.0)
        dn = jnp.where(has_next, an_ref[:, cs], 0.0)
        slabs = [jnp.concatenate([up, a_ref[0:tm - gw, cs]], axis=0), a_ref[:, cs],
                 jnp.concatenate([a_ref[gw:tm, cs], dn], axis=0)]

        def vsum(dj):
            k = lambda di: 3 * (di + 1) + (dj + 1)
            return (slabs[0] * (cw_ref[k(-1):k(-1) + 1, cs] * vert)
                    + slabs[1] * cw_ref[k(0):k(0) + 1, cs]
                    + slabs[2] * (cw_ref[k(1):k(1) + 1, cs] * vert))

        side = jnp.concatenate([vsum(-1), vsum(1)], axis=0).astype(BF16)
        return vsum(0) + cb_ref[:, cs], _dot(shift, side)

    def finish(c0, centre, sides, acc_out):
        cs = slice(c0, c0 + cw)
        mid = (_silu(centre + sides) * gl_ref[:, cs]).astype(BF16)
        part = _dot(mid, w_ref[cs, :])
        return part if acc_out is None else acc_out + part

    acc_out = None
    pending = None
    for c0 in range(0, f, cw):
        cur = (c0,) + taps(c0)
        if pending is not None:
            acc_out = finish(*pending, acc_out)
        pending = cur
    acc_out = finish(*pending, acc_out)
    out = x_ref[...] + g_ref[...] * acc_out
    if final_norm:
        out = _rms_scale(out) * fw_ref[...]
    o_ref[...] = out


def _ffn_down(a, gl, xres, mods, layer, conv_w9, conv_b, w_down, *, n_ctx_tiles, ctx_row,
              final_w=None):
    bsz, tt, f = a.shape
    d = xres.shape[-1]
    tm = TOKEN_TILE
    gw = GRID_WIDTH
    assert f % FFN_COL_CHUNK == 0
    n_tiles = tt // tm
    hb = tm // gw
    row_fn = lambda b, j: jnp.where(j < n_ctx_tiles, ctx_row, b)
    tok = lambda c: pl.BlockSpec((None, tm, c), lambda b, j: (b, j, 0))
    in_specs = [
        tok(f),
        pl.BlockSpec((None, gw, f), lambda b, j: (b, jnp.maximum(j * hb - 1, 0), 0)),
        pl.BlockSpec((None, gw, f), lambda b, j: (b, jnp.minimum((j + 1) * hb, tt // gw - 1), 0)),
        tok(f), tok(d), _mod_spec(d, layer, 5, row_fn), _full(conv_w9), _full(conv_b),
        _layer_spec(w_down, layer)]
    args = [a, a, a, gl, xres, mods, conv_w9, conv_b, w_down]
    if final_w is not None:
        in_specs.append(_full(final_w))
        args.append(final_w)
    kern = functools.partial(_ffn_down_kernel, n_ctx_tiles=n_ctx_tiles, tiles_total=n_tiles,
                             final_norm=final_w is not None)
    return pl.pallas_call(
        kern, grid=(bsz, n_tiles), in_specs=in_specs, out_specs=tok(d),
        out_shape=jax.ShapeDtypeStruct((bsz, tt, d), F32),
        compiler_params=_params(2), name=f"ffn_down_l{layer}",
    )(*args)


def _pad_cols(a, n):
    return jnp.pad(a, [(0, 0)] * (a.ndim - 1) + [(0, n - a.shape[-1])])


def kernel(x, c, ctx, c_ctx, ada_w, ada_b, norm1_w, norm2_w, ssm_w_in, ssm_conv_w, ssm_conv_b, ssm_dt_bias, ssm_a_log, ssm_d, ssm_norm_w, ssm_w_out, mlstm_w_in, mlstm_conv_w, mlstm_conv_b, mlstm_gate_b, mlstm_norm_w, mlstm_w_out, ffn_w_up, ffn_conv_w, ffn_conv_b, ffn_w_down, final_norm_w):
    bsz, seq, d = x.shape
    n_ctx = ctx.shape[1]
    depth = ada_w.shape[0]
    assert depth == 2 and ssm_w_in.shape[0] == 1 and mlstm_w_in.shape[0] == 1
    assert seq % TOKEN_TILE == 0 and n_ctx == TOKEN_TILE and TOKEN_TILE % GRID_WIDTH == 0
    n_ctx_tiles = n_ctx // TOKEN_TILE
    n_ctx_chunks = n_ctx // CHUNK
    ctx_row = bsz

    n_rows = -(-(bsz + 1) // SUBLANES) * SUBLANES
    cvec = jnp.zeros((n_rows, d), F32).at[:bsz].set(c).at[bsz].set(c_ctx)
    mods = _ada(cvec, ada_w, ada_b).reshape(depth, n_rows, 6, 1, d)

    w_up = ffn_w_up.astype(BF16)
    w_down = ffn_w_down.astype(BF16)
    f = ffn_conv_b.shape[1]
    conv9 = ffn_conv_w.reshape(depth, 9, f)

    heads = ssm_d.shape[1]
    assert 2 * heads <= LANES
    inner = heads * SSM_HEAD_DIM
    bc_w = 2 * SSM_GROUPS * SSM_STATE
    conv_ch = inner + bc_w
    w_in = _pad_cols(ssm_w_in, 2 * inner + bc_w + LANES).astype(BF16)
    splits = [(0, inner), (inner, conv_ch), (inner + conv_ch, LANES)]
    z, xbc, dt = _project(
        (ctx, x), norm1_w[0:1], mods, 0, 0, 1, w_in, splits, (BF16, BF16, F32),
        conv=(1, ssm_conv_w[0], ssm_conv_b[0:1]), n_ctx_tiles=n_ctx_tiles, ctx_row=ctx_row,
        name="in_proj_l0")
    dtb = _pad_cols(ssm_dt_bias[0].reshape(1, -1), LANES)
    alog = _pad_cols(ssm_a_log[0].reshape(1, -1), LANES)
    y_f, y_b = _ssd_scan(xbc, dt, dtb, alog, heads=heads, n_ctx_chunks=n_ctx_chunks)
    d_row = jnp.repeat(ssm_d[0], SSM_HEAD_DIM).reshape(1, inner)
    xall = _ssm_out(y_f, y_b, xbc, z, (ctx, x), mods, 0, d_row, ssm_norm_w[0:1],
                    ssm_w_out.astype(BF16), n_ctx_tiles=n_ctx_tiles, ctx_row=ctx_row)

    a, gl = _project((xall,), norm2_w[0:1], mods, 0, 3, 4, w_up, [(0, f), (f, f)], (F32, F32),
                     n_ctx_tiles=n_ctx_tiles, ctx_row=ctx_row, name="ffn_up_l0")
    xall = _ffn_down(a, gl, xall, mods, 0, conv9[0], ffn_conv_b[0:1], w_down,
                     n_ctx_tiles=n_ctx_tiles, ctx_row=ctx_row)

    qkw = mlstm_conv_b.shape[1]
    vw = mlstm_norm_w.shape[1]
    w_in = _pad_cols(mlstm_w_in, qkw + 2 * vw + LANES).astype(BF16)
    splits = [(0, qkw), (qkw, vw), (qkw + vw, vw), (qkw + 2 * vw, LANES)]
    qk, v, o_gate, gates = _project(
        (xall,), norm1_w[1:2], mods, 1, 0, 1, w_in, splits, (BF16, BF16, BF16, F32),
        conv=(0, mlstm_conv_w[0], mlstm_conv_b[0:1]), n_ctx_tiles=n_ctx_tiles, ctx_row=ctx_row,
        name="in_proj_l1")
    gate_b = _pad_cols(mlstm_gate_b[0:1], LANES)
    h_f, h_b = _mlstm_scan(qk, v, gates, gate_b, n_ctx_chunks=n_ctx_chunks)
    xlat = _mlstm_out(h_f, h_b, o_gate, xall, mods, 1, mlstm_norm_w[0:1],
                      mlstm_w_out.astype(BF16), n_ctx_tiles=n_ctx_tiles)

    a, gl = _project((xlat,), norm2_w[1:2], mods, 1, 3, 4, w_up, [(0, f), (f, f)], (F32, F32),
                     n_ctx_tiles=0, ctx_row=ctx_row, name="ffn_up_l1")
    return _ffn_down(a, gl, xlat, mods, 1, conv9[1], ffn_conv_b[1:2], w_down,
                     n_ctx_tiles=0, ctx_row=ctx_row, final_w=final_norm_w.reshape(1, d))
```

```python
import functools

import jax
import jax.numpy as jnp
from jax import lax
from jax.experimental import pallas as pl
from jax.experimental.pallas import tpu as pltpu

F32 = jnp.float32
BF16 = jnp.bfloat16
HIGHEST = lax.Precision.HIGHEST

NORM_EPS = 1e-6
GRID_WIDTH = 64
SSM_HEAD_DIM = 64
SSM_GROUPS = 4
SSM_STATE = 128
MLSTM_HEADS = 4
GATE_CAP = 15.0
CHUNK = 128

LANES = 128
SUBLANES = 8
TOKEN_TILE = 512
COL_CHUNK = 512
FFN_COL_CHUNK = 256
FFN_SHIFT_BLOCK = 256
VMEM_LIMIT = 56 * 1024 * 1024


class _Layout:
    def __init__(self, bsz, n_ctx, seq, tm=TOKEN_TILE):
        assert tm % n_ctx == 0 and (bsz * n_ctx) % tm == 0 and seq % tm == 0
        assert n_ctx % CHUNK == 0 and seq % CHUNK == 0 and seq % GRID_WIDTH == 0
        self.bsz, self.n_ctx, self.seq, self.tm = bsz, n_ctx, seq, tm
        self.ctx_rows, self.lat_rows = bsz * n_ctx, bsz * seq
        self.rows = self.ctx_rows + self.lat_rows
        self.ctx_tiles = self.ctx_rows // tm
        self.lat_tiles_per_batch = seq // tm
        self.lat_tiles = bsz * self.lat_tiles_per_batch
        self.tiles = self.ctx_tiles + self.lat_tiles
        self.ctx_chunks, self.lat_chunks = n_ctx // CHUNK, seq // CHUNK

    def mod_row(self, j):
        return jnp.where(j < self.ctx_tiles, self.bsz, (j - self.ctx_tiles) // self.lat_tiles_per_batch)

    def batch_pos(self, j):
        return (j - self.ctx_tiles) % self.lat_tiles_per_batch


def _params(n_axes=1):
    return pltpu.CompilerParams(
        dimension_semantics=("arbitrary",) * n_axes, vmem_limit_bytes=VMEM_LIMIT)


def _col_chunks(start, width, step=COL_CHUNK):
    out, c = [], start
    while c < start + width:
        w = min(step, start + width - c)
        out.append((c, w))
        c += w
    return out


def _silu(v):
    return v * jax.nn.sigmoid(v)


def _softplus(v):
    return jnp.maximum(v, 0.0) + jnp.log1p(jnp.exp(-jnp.abs(v)))


def _rms_scale(v):
    return v * lax.rsqrt(jnp.mean(v * v, axis=-1, keepdims=True) + NORM_EPS)


def _dot(a, b):
    return jnp.dot(a, b, preferred_element_type=F32)


def _dot_nt(a, b):
    return lax.dot_general(a, b, (((1,), (1,)), ((), ())), preferred_element_type=F32)


def _full(a):
    return pl.BlockSpec(a.shape, lambda *_: (0,) * a.ndim)


def _layer_spec(a, layer):
    return pl.BlockSpec((None,) + a.shape[1:], lambda *_: (layer,) + (0,) * (a.ndim - 1),
                        pipeline_mode=pl.Buffered(1))


def _mod_spec(d, layer, k, row_fn):
    return pl.BlockSpec((None, None, None, 1, d), lambda j: (layer, row_fn(j), k, 0, 0))


def _rows_spec(rows, width, idx_fn):
    return pl.BlockSpec((rows, width), lambda j: (idx_fn(j), 0))


def _ada_kernel(c_ref, w_ref, b_ref, o_ref):
    s = _silu(c_ref[...])
    o_ref[...] = jnp.dot(s, w_ref[...], precision=HIGHEST, preferred_element_type=F32) + b_ref[...]


def _ada(cvec, ada_w, ada_b):
    depth, d, n = ada_w.shape
    rows = cvec.shape[0]
    tn = 512
    return pl.pallas_call(
        _ada_kernel,
        grid=(depth, n // tn),
        in_specs=[
            pl.BlockSpec((rows, d), lambda l, j: (0, 0)),
            pl.BlockSpec((None, d, tn), lambda l, j: (l, 0, j)),
            pl.BlockSpec((None, 1, tn), lambda l, j: (l, 0, j)),
        ],
        out_specs=pl.BlockSpec((None, rows, tn), lambda l, j: (l, 0, j)),
        out_shape=jax.ShapeDtypeStruct((depth, rows, n), F32),
        compiler_params=_params(2),
        name="ada_mod",
    )(cvec, ada_w, ada_b.reshape(depth, 1, n))


def _tok_specs(lay, arrs, rows, shift=0):
    per = lay.tm // rows
    if len(arrs) == 1:
        (a,) = arrs
        hi = a.shape[0] // rows - 1
        return [_rows_spec(rows, a.shape[1], lambda j: jnp.clip(j * per + shift, 0, hi))]
    ctx, lat = arrs
    hi_c = ctx.shape[0] // rows - 1
    hi_l = lat.shape[0] // rows - 1
    return [_rows_spec(rows, ctx.shape[1], lambda j: jnp.clip(j * per + shift, 0, hi_c)),
            _rows_spec(rows, lat.shape[1],
                       lambda j: jnp.clip((j - lay.ctx_tiles) * per + shift, 0, hi_l))]


def _tok_read(refs, is_ctx):
    if len(refs) == 1:
        return refs[0][...]
    return jnp.where(is_ctx, refs[0][...], refs[1][...])


def _proj_kernel(*refs, lay, n_src, splits, conv_idx, skip):
    refs = list(refs)
    x_refs = [refs.pop(0) for _ in range(n_src)]
    if conv_idx is not None:
        xp_refs = [refs.pop(0) for _ in range(n_src)]
        xn_refs = [refs.pop(0) for _ in range(n_src)]
    nw_ref, sh_ref, sc_ref, w_ref = refs[:4]
    refs = refs[4:]
    if conv_idx is not None:
        cw_ref, cb_ref = refs[:2]
        refs = refs[2:]
    outs = refs
    j = pl.program_id(0) + skip
    is_ctx = j < lay.ctx_tiles
    tm = lay.tm

    def prep(v):
        h = _rms_scale(v) * nw_ref[...]
        return h * (1.0 + sc_ref[...]) + sh_ref[...]

    h_main = prep(_tok_read(x_refs, is_ctx))
    hb = h_main.astype(BF16)
    if conv_idx is not None:
        pos = lay.batch_pos(j)
        is_first = jnp.logical_or(is_ctx, pos == 0)
        is_last = jnp.logical_or(is_ctx, pos == lay.lat_tiles_per_batch - 1)
        hp = jnp.where(is_first, 0.0, prep(_tok_read(xp_refs, is_ctx)))
        hn = jnp.where(is_last, 0.0, prep(_tok_read(xn_refs, is_ctx)))
        hext = jnp.concatenate([hp, h_main, hn], axis=0).astype(BF16)
        te = tm + 2 * SUBLANES
        seam_rows = lax.broadcasted_iota(jnp.int32, (te, 1), 0) - SUBLANES
        keep_up = jnp.where(jnp.logical_and(is_ctx, seam_rows % lay.n_ctx == 0), 0.0, 1.0)
        keep_dn = jnp.where(jnp.logical_and(is_ctx, seam_rows % lay.n_ctx == lay.n_ctx - 1), 0.0, 1.0)
    for oi, (s0, sw) in enumerate(splits):
        for (c0, cw) in _col_chunks(s0, sw):
            if oi == conv_idx:
                r = _dot(hext, w_ref[:, c0:c0 + cw])
                k0 = c0 - s0

                def conv(up, dn):
                    y = (up * cw_ref[0:1, k0:k0 + cw] + r * cw_ref[1:2, k0:k0 + cw]
                         + dn * cw_ref[2:3, k0:k0 + cw] + cb_ref[:, k0:k0 + cw])
                    return _silu(y[SUBLANES:SUBLANES + tm, :])

                up, dn = pltpu.roll(r, 1, 0), pltpu.roll(r, te - 1, 0)
                r = conv(up * keep_up, dn * keep_dn)
            else:
                r = _dot(hb, w_ref[:, c0:c0 + cw])
            outs[oi][:, c0 - s0:c0 - s0 + cw] = r.astype(outs[oi].dtype)


def _project(lay, srcs, norm_w, mods, layer, k_shift, k_scale, w, splits, out_dtypes, *,
             conv=None, skip=0, name):
    d = srcs[0].shape[1]
    tm = lay.tm
    n_tiles = lay.tiles - skip
    row_fn = lambda j: lay.mod_row(j + skip)
    in_specs = _tok_specs(lay, srcs, tm)
    args = list(srcs)
    conv_idx = None
    if conv is not None:
        assert skip == 0
        conv_idx, conv_w, conv_b = conv
        in_specs += _tok_specs(lay, srcs, SUBLANES, shift=-1)
        in_specs += _tok_specs(lay, srcs, SUBLANES, shift=tm // SUBLANES)
        args += list(srcs) * 2
    in_specs += [_full(norm_w), _mod_spec(d, layer, k_shift, row_fn),
                 _mod_spec(d, layer, k_scale, row_fn), _layer_spec(w, layer if w.shape[0] > 1 else 0)]
    args += [norm_w, mods, mods, w]
    if conv is not None:
        in_specs += [_full(conv_w), _full(conv_b)]
        args += [conv_w, conv_b]
    out_specs = [_rows_spec(tm, sw, lambda j: j) for (_, sw) in splits]
    out_shape = [jax.ShapeDtypeStruct((n_tiles * tm, sw), dt) for (_, sw), dt in zip(splits, out_dtypes)]
    kern = functools.partial(_proj_kernel, lay=lay, n_src=len(srcs), splits=tuple(splits),
                             conv_idx=conv_idx, skip=skip)
    return pl.pallas_call(
        kern, grid=(n_tiles,), in_specs=in_specs, out_specs=out_specs, out_shape=out_shape,
        compiler_params=_params(), name=name,
    )(*args)


def _ssd_prep(dt_ref, dtb_ref, alog_ref, *, reverse):
    q = CHUNK
    r_i = lax.broadcasted_iota(jnp.int32, (q, q), 0)
    c_i = lax.broadcasted_iota(jnp.int32, (q, q), 1)
    if reverse:
        tri = (c_i >= r_i)
        mask_sl = (r_i >= c_i)
    else:
        tri = (c_i <= r_i)
        mask_sl = (r_i <= c_i)

    dts_t = _softplus(dt_ref[...] + dtb_ref[...])
    la_t = dts_t * (-jnp.exp(alog_ref[...]))
    w_t = jnp.dot(tri.astype(F32), la_t, precision=HIGHEST, preferred_element_type=F32)
    w = w_t.T
    dts = dts_t.T
    tot = w[:, 0:1] if reverse else w[:, q - 1:q]
    dte = jnp.exp(tot - w)
    ew = jnp.exp(w)
    etot = jnp.exp(tot)
    return dict(mask_sl=mask_sl, w=w, w_t=w_t, dts=dts, dte=dte, ew=ew, etot=etot)


def _ssd_group(p, xbc_ref, st_ref, g, *, reverse, heads):
    hd, ng, ns = SSM_HEAD_DIM, SSM_GROUPS, SSM_STATE
    hpg = heads // ng
    inner = heads * hd
    dir_off = heads if reverse else 0
    w, w_t = p["w"], p["w_t"]
    bb = xbc_ref[:, inner + g * ns: inner + (g + 1) * ns]
    cb = xbc_ref[:, inner + ng * ns + g * ns: inner + ng * ns + (g + 1) * ns]
    cbt = _dot_nt(bb, cb)
    st_g = st_ref[g]
    yint = _dot_nt(st_g.astype(BF16), cb)
    y_parts, xdec_parts = [], []
    for pr in range(hpg // 2):
        col = (g * hpg + 2 * pr) * hd
        blk_t = xbc_ref[:, col:col + 2 * hd].astype(F32).T
        for half in range(2):
            r = 2 * pr + half
            hh = dir_off + g * hpg + r
            xd = blk_t[half * hd:(half + 1) * hd, :] * p["dts"][hh:hh + 1, :]
            expo = jnp.where(p["mask_sl"], w[hh:hh + 1, :] - w_t[:, hh:hh + 1], -jnp.inf)
            mt = (cbt * jnp.exp(expo)).astype(BF16)
            y_h = (_dot(xd.astype(BF16), mt)
                   + yint[r * hd:(r + 1) * hd, :] * p["ew"][hh:hh + 1, :])
            y_parts.append(y_h)
            xdec_parts.append((xd * p["dte"][hh:hh + 1, :]).astype(BF16))
    xdec = jnp.concatenate(xdec_parts, axis=0)
    sloc = _dot(xdec, bb)
    new_state = [st_g[r * hd:(r + 1) * hd, :] * p["etot"][dir_off + g * hpg + r:dir_off + g * hpg + r + 1, :]
                 + sloc[r * hd:(r + 1) * hd, :] for r in range(hpg)]
    y_blocks = [jnp.concatenate(y_parts[2 * pr:2 * pr + 2], axis=0).T for pr in range(hpg // 2)]
    return y_blocks, new_state


def _ssd_kernel(xf_ref, dtf_ref, xb_ref, dtb2_ref, dtb_ref, alog_ref, yf_ref, yb_ref,
                stf_ref, stb_ref, *, heads):
    @pl.when(pl.program_id(1) == 0)
    def _():
        stf_ref[...] = jnp.zeros_like(stf_ref)
        stb_ref[...] = jnp.zeros_like(stb_ref)

    hd = SSM_HEAD_DIM
    hpg = heads // SSM_GROUPS
    chains = [
        (_ssd_prep(dtf_ref, dtb_ref, alog_ref, reverse=False), xf_ref, stf_ref, yf_ref, False),
        (_ssd_prep(dtb2_ref, dtb_ref, alog_ref, reverse=True), xb_ref, stb_ref, yb_ref, True),
    ]
    for g in range(SSM_GROUPS):
        results = [_ssd_group(p, x_ref, st_ref, g, reverse=rev, heads=heads)
                   for (p, x_ref, st_ref, _, rev) in chains]
        for (_, _, st_ref, y_ref, _), (y_blocks, new_state) in zip(chains, results):
            for r, s_new in enumerate(new_state):
                st_ref[g, r * hd:(r + 1) * hd, :] = s_new
            for pr, blk in enumerate(y_blocks):
                col = (g * hpg + 2 * pr) * hd
                y_ref[:, col:col + 2 * hd] = blk.astype(y_ref.dtype)


def _chunk_block(lay, reverse):
    ncc, nlc = lay.ctx_chunks, lay.lat_chunks

    def block(b, i):
        in_ctx = i < ncc
        c_ctx = (ncc - 1 - i) if reverse else i
        c_lat = (nlc - 1 - (i - ncc)) if reverse else (i - ncc)
        return jnp.where(in_ctx, b * ncc + c_ctx, lay.bsz * ncc + b * nlc + c_lat)

    return block


def _ssd_scan(lay, xbc, dt, dtb, alog, *, heads):
    cch = xbc.shape[1]
    inner = heads * SSM_HEAD_DIM
    fwd, bwd = _chunk_block(lay, False), _chunk_block(lay, True)
    blk = lambda width, order: pl.BlockSpec((CHUNK, width), lambda b, i: (order(b, i), 0))
    state = pltpu.VMEM((SSM_GROUPS, inner // SSM_GROUPS, SSM_STATE), F32)
    return pl.pallas_call(
        functools.partial(_ssd_kernel, heads=heads),
        grid=(lay.bsz, lay.ctx_chunks + lay.lat_chunks),
        in_specs=[blk(cch, fwd), blk(LANES, fwd), blk(cch, bwd), blk(LANES, bwd),
                  pl.BlockSpec((1, LANES), lambda b, i: (0, 0)),
                  pl.BlockSpec((1, LANES), lambda b, i: (0, 0))],
        out_specs=[blk(inner, fwd), blk(inner, bwd)],
        out_shape=[jax.ShapeDtypeStruct((lay.rows, inner), BF16)] * 2,
        scratch_shapes=[state, state],
        compiler_params=_params(2),
        name="ssd_scan",
    )(xbc, dt, xbc, dt, dtb, alog)


def _ssm_out_kernel(*refs, lay):
    yf_ref, yb_ref, xs_ref, z_ref, xc_ref, xl_ref, g_ref, d_ref, nw_ref, w_ref, o_ref = refs
    is_ctx = pl.program_id(0) < lay.ctx_tiles
    z = z_ref[...].astype(F32)
    y = (yf_ref[...].astype(F32) + yb_ref[...].astype(F32)
         + d_ref[...] * xs_ref[...].astype(F32)) * _silu(z)
    yn = (_rms_scale(y) * nw_ref[...]).astype(BF16)
    o_ref[...] = _tok_read([xc_ref, xl_ref], is_ctx) + g_ref[...] * _dot(yn, w_ref[...])


def _ssm_out(lay, yf, yb, xbc, z, srcs, mods, layer, d_row, norm_w, w_out):
    inner = yf.shape[1]
    d = srcs[0].shape[1]
    tm = lay.tm
    tok = lambda c: _rows_spec(tm, c, lambda j: j)
    return pl.pallas_call(
        functools.partial(_ssm_out_kernel, lay=lay), grid=(lay.tiles,),
        in_specs=[tok(inner), tok(inner), tok(inner), tok(inner)]
        + _tok_specs(lay, srcs, tm)
        + [_mod_spec(d, layer, 2, lay.mod_row), _full(d_row), _full(norm_w), _layer_spec(w_out, 0)],
        out_specs=tok(d),
        out_shape=jax.ShapeDtypeStruct((lay.rows, d), F32),
        compiler_params=_params(), name="ssm_out",
    )(yf, yb, xbc, z, *srcs, mods, d_row, norm_w, w_out)


def _mlstm_prep(gt_ref, gb_ref, m_ref, *, reverse):
    q = CHUNK
    nh = MLSTM_HEADS
    d = 1 if reverse else 0

    r_i = lax.broadcasted_iota(jnp.int32, (q, q), 0)
    c_i = lax.broadcasted_iota(jnp.int32, (q, q), 1)
    keep = (c_i >= r_i) if reverse else (c_i <= r_i)

    gt = GATE_CAP * jnp.tanh((gt_ref[...] + gb_ref[...]) / GATE_CAP)
    lane_ok = c_i < nh
    ig_sh = (LANES - nh * d) % LANES
    fg_sh = LANES - (2 * nh + nh * d)
    ig_t = gt if ig_sh == 0 else pltpu.roll(gt, ig_sh, 1)
    ig_t = jnp.where(lane_ok, ig_t, 0.0)
    fr_t = pltpu.roll(gt, fg_sh, 1)
    lf_t = jnp.where(lane_ok, -_softplus(-fr_t), 0.0)
    wv_t = jnp.dot(keep.astype(F32), lf_t, precision=HIGHEST, preferred_element_type=F32)
    ftot = jnp.sum(lf_t, axis=0, keepdims=True)
    wend_t = ftot - wv_t + ig_t
    mloc = jnp.max(wend_t, axis=0, keepdims=True)
    e_t = jnp.exp(wend_t - mloc)
    m_prev = m_ref[...]
    m_new = jnp.maximum(ftot + m_prev, mloc)
    a_prev = jnp.exp(ftot + m_prev - m_new)
    a_loc = jnp.exp(mloc - m_new)
    run = ig_t - wv_t
    sh = 1
    while sh < q:
        if reverse:
            moved, ok = pltpu.roll(run, q - sh, 0), r_i < q - sh
        else:
            moved, ok = pltpu.roll(run, sh, 0), r_i >= sh
        run = jnp.maximum(run, jnp.where(ok, moved, -jnp.inf))
        sh *= 2
    gcol_t = wv_t + m_prev
    mcomb_t = jnp.maximum(gcol_t, wv_t + run)
    wv = wv_t.T
    return dict(keep=keep, wm_t=wv_t - mcomb_t, rowterm=ig_t.T - wv, inter_t=jnp.exp(gcol_t - mcomb_t),
                floor_t=jnp.exp(-mcomb_t), e_r=e_t.T, a_prev=a_prev, a_loc=a_loc, m_new=m_new)


def _mlstm_head(p, qk_ref, v_ref, c_ref, h):
    q = CHUNK
    nh = MLSTM_HEADS
    dk = qk_ref.shape[-1] // (2 * nh)
    dv = v_ref.shape[-1] // nh
    qb = qk_ref[:, h * dk:(h + 1) * dk]
    kt = (qk_ref[:, nh * dk + h * dk: nh * dk + (h + 1) * dk].astype(F32) * (dk ** -0.5)).T
    vaug = jnp.concatenate([v_ref[:, h * dv:(h + 1) * dv], jnp.ones((q, LANES), BF16)], axis=1)
    caug = c_ref[h]
    lanes = lambda col: jnp.broadcast_to(col, (q, LANES))
    pm = jnp.exp(jnp.where(p["keep"], p["wm_t"][:, h:h + 1] + p["rowterm"][h:h + 1, :], -jnp.inf))
    inter = lanes(p["inter_t"][:, h:h + 1])
    floor = lanes(p["floor_t"][:, h:h + 1])
    s = (_dot(qb, kt.astype(BF16)) * pm).astype(BF16)
    sv = _dot(s, vaug)
    qc = _dot(qb, caug.astype(BF16))
    blocks = [sv[:, c:c + LANES] + inter * qc[:, c:c + LANES] for c in range(0, dv + LANES, LANES)]
    rden = 1.0 / jnp.maximum(jnp.abs(blocks[-1]), floor)
    h_out = jnp.concatenate([b * rden for b in blocks[:-1]], axis=1)
    ap = p["a_prev"][:, h:h + 1]
    al = p["a_loc"][:, h:h + 1]
    c_new = ap * caug + al * _dot((kt * p["e_r"][h:h + 1, :]).astype(BF16), vaug)
    return h_out, c_new


def _mlstm_kernel(qkf_ref, vf_ref, gtf_ref, qkb_ref, vb_ref, gtb_ref, gb_ref, hf_ref, hb_ref,
                  cf_ref, mf_ref, cb_ref, mb_ref):
    @pl.when(pl.program_id(1) == 0)
    def _():
        for r in (cf_ref, mf_ref, cb_ref, mb_ref):
            r[...] = jnp.zeros_like(r)

    dv = vf_ref.shape[-1] // MLSTM_HEADS
    for (gt_ref, qk_ref, v_ref, c_ref, m_ref, h_ref, rev) in (
            (gtf_ref, qkf_ref, vf_ref, cf_ref, mf_ref, hf_ref, False),
            (gtb_ref, qkb_ref, vb_ref, cb_ref, mb_ref, hb_ref, True)):
        p = _mlstm_prep(gt_ref, gb_ref, m_ref, reverse=rev)
        m_ref[...] = p["m_new"]
        for h in range(MLSTM_HEADS):
            h_out, c_new = _mlstm_head(p, qk_ref, v_ref, c_ref, h)
            h_ref[:, h * dv:(h + 1) * dv] = h_out.astype(h_ref.dtype)
            c_ref[h] = c_new


def _mlstm_scan(lay, qk, v, gates, gate_b):
    qkw, vw = qk.shape[1], v.shape[1]
    nh = MLSTM_HEADS
    dk, dv = qkw // (2 * nh), vw // nh
    ncc, nlc = lay.ctx_chunks, lay.lat_chunks
    fwd, bwd = _chunk_block(lay, False), _chunk_block(lay, True)
    out_f = lambda b, i: b * nlc + jnp.maximum(i - ncc, 0)
    out_b = lambda b, i: b * nlc + nlc - 1 - jnp.maximum(i - ncc, 0)
    blk = lambda width, order: pl.BlockSpec((CHUNK, width), lambda b, i: (order(b, i), 0))
    ins = lambda order: [blk(qkw, order), blk(vw, order), blk(LANES, order)]
    state = [pltpu.VMEM((nh, dk, dv + LANES), F32), pltpu.VMEM((1, LANES), F32)]
    return pl.pallas_call(
        _mlstm_kernel,
        grid=(lay.bsz, ncc + nlc),
        in_specs=ins(fwd) + ins(bwd) + [pl.BlockSpec((1, LANES), lambda b, i: (0, 0))],
        out_specs=[blk(vw, out_f), blk(vw, out_b)],
        out_shape=[jax.ShapeDtypeStruct((lay.lat_rows, vw), BF16)] * 2,
        scratch_shapes=state + state,
        compiler_params=_params(2),
        name="mlstm_scan",
    )(qk, v, gates, qk, v, gates, gate_b)


def _mlstm_out_kernel(hf_ref, hb_ref, o_ref_in, x_ref, g_ref, nw_ref, w_ref, out_ref):
    nh = MLSTM_HEADS
    dv = hf_ref.shape[-1] // nh
    parts = []
    for h in range(nh):
        hh = (hf_ref[:, h * dv:(h + 1) * dv].astype(F32) + hb_ref[:, h * dv:(h + 1) * dv].astype(F32))
        parts.append(_rms_scale(hh))
    hn = jnp.concatenate(parts, axis=1)
    y = (hn * nw_ref[...] * jax.nn.sigmoid(o_ref_in[...].astype(F32))).astype(BF16)
    out_ref[...] = x_ref[...] + g_ref[...] * _dot(y, w_ref[...])


def _mlstm_out(lay, hf, hb, o_gate, xall, mods, layer, norm_w, w_out):
    vw = hf.shape[1]
    d = xall.shape[1]
    tm = lay.tm
    lat = lambda c: _rows_spec(tm, c, lambda j: j)
    allt = lambda c: _rows_spec(tm, c, lambda j: j + lay.ctx_tiles)
    return pl.pallas_call(
        _mlstm_out_kernel, grid=(lay.lat_tiles,),
        in_specs=[lat(vw), lat(vw), allt(vw), allt(d),
                  _mod_spec(d, layer, 2, lambda j: j // lay.lat_tiles_per_batch),
                  _full(norm_w), _layer_spec(w_out, 0)],
        out_specs=lat(d),
        out_shape=jax.ShapeDtypeStruct((lay.lat_rows, d), F32),
        compiler_params=_params(), name="mlstm_out",
    )(hf, hb, o_gate, xall, mods, norm_w, w_out)


def _ffn_down_kernel(*refs, lay, skip, final_norm):
    if final_norm:
        a_ref, ap_ref, an_ref, gl_ref, x_ref, g_ref, cw_ref, cb_ref, w_ref, fw_ref, o_ref = refs
    else:
        a_ref, ap_ref, an_ref, gl_ref, x_ref, g_ref, cw_ref, cb_ref, w_ref, o_ref = refs
    tm, f = a_ref.shape
    gw = GRID_WIDTH
    j = pl.program_id(0) + skip
    is_ctx = j < lay.ctx_tiles
    pos = lay.batch_pos(j)
    has_prev = jnp.logical_and(jnp.logical_not(is_ctx), pos > 0)
    has_next = jnp.logical_and(jnp.logical_not(is_ctx), pos < lay.lat_tiles_per_batch - 1)

    cw = FFN_COL_CHUNK
    vert = jnp.where(is_ctx, 0.0, 1.0)
    sb = FFN_SHIFT_BLOCK
    ri = lax.broadcasted_iota(jnp.int32, (sb, sb), 0)
    ci = lax.broadcasted_iota(jnp.int32, (sb, sb), 1)
    colpos = jnp.where(is_ctx, ri % lay.n_ctx, ri & (gw - 1))
    last_col = jnp.where(is_ctx, lay.n_ctx - 1, gw - 1)
    take_left = jnp.where(colpos != 0, jnp.where(ci == ri - 1, 1.0, 0.0), 0.0)
    take_right = jnp.where(colpos != last_col, jnp.where(ci == ri + 1, 1.0, 0.0), 0.0)
    shift = jnp.concatenate([take_left, take_right], axis=1).astype(BF16)

    def taps(c0):
        cs = slice(c0, c0 + cw)
        up = jnp.where(has_prev, ap_ref[:, cs], 0.0)
        dn = jnp.where(has_next, an_ref[:, cs], 0.0)
        slabs = [jnp.concatenate([up, a_ref[0:tm - gw, cs]], axis=0), a_ref[:, cs],
                 jnp.concatenate([a_ref[gw:tm, cs], dn], axis=0)]

        def vsum(dj):
            k = lambda di: 3 * (di + 1) + (dj + 1)
            return (slabs[0] * (cw_ref[k(-1):k(-1) + 1, cs] * vert)
                    + slabs[1] * cw_ref[k(0):k(0) + 1, cs]
                    + slabs[2] * (cw_ref[k(1):k(1) + 1, cs] * vert))

        left, right = vsum(-1).astype(BF16), vsum(1).astype(BF16)
        sides = jnp.concatenate(
            [_dot(shift, jnp.concatenate([left[r0:r0 + sb], right[r0:r0 + sb]], axis=0))
             for r0 in range(0, tm, sb)], axis=0)
        return vsum(0) + cb_ref[:, cs], sides

    def finish(c0, centre, sides, acc_out):
        cs = slice(c0, c0 + cw)
        mid = (_silu(centre + sides) * gl_ref[:, cs]).astype(BF16)
        part = _dot(mid, w_ref[cs, :])
        return part if acc_out is None else acc_out + part

    acc_out = None
    pending = None
    for c0 in range(0, f, cw):
        cur = (c0,) + taps(c0)
        if pending is not None:
            acc_out = finish(*pending, acc_out)
        pending = cur
    acc_out = finish(*pending, acc_out)
    out = x_ref[...] + g_ref[...] * acc_out
    if final_norm:
        out = _rms_scale(out) * fw_ref[...]
    o_ref[...] = out


def _ffn_down(lay, a, gl, xres, mods, layer, conv_w9, conv_b, w_down, *, skip=0, final_w=None):
    tt, f = a.shape
    d = xres.shape[1]
    tm = lay.tm
    gw = GRID_WIDTH
    assert f % FFN_COL_CHUNK == 0 and tm % FFN_SHIFT_BLOCK == 0
    assert FFN_SHIFT_BLOCK % gw == 0 and FFN_SHIFT_BLOCK % lay.n_ctx == 0
    n_tiles = tt // tm
    hb = tm // gw
    tok = lambda c: _rows_spec(tm, c, lambda j: j)
    in_specs = [
        tok(f),
        _rows_spec(gw, f, lambda j: jnp.maximum(j * hb - 1, 0)),
        _rows_spec(gw, f, lambda j: jnp.minimum((j + 1) * hb, tt // gw - 1)),
        tok(f), tok(d), _mod_spec(d, layer, 5, lambda j: lay.mod_row(j + skip)),
        _full(conv_w9), _full(conv_b), _layer_spec(w_down, layer)]
    args = [a, a, a, gl, xres, mods, conv_w9, conv_b, w_down]
    if final_w is not None:
        in_specs.append(_full(final_w))
        args.append(final_w)
    kern = functools.partial(_ffn_down_kernel, lay=lay, skip=skip, final_norm=final_w is not None)
    return pl.pallas_call(
        kern, grid=(n_tiles,), in_specs=in_specs, out_specs=tok(d),
        out_shape=jax.ShapeDtypeStruct((tt, d), F32),
        compiler_params=_params(), name=f"ffn_down_l{layer}",
    )(*args)


def _pad_cols(a, n):
    return jnp.pad(a, [(0, 0)] * (a.ndim - 1) + [(0, n - a.shape[-1])])


def kernel(x, c, ctx, c_ctx, ada_w, ada_b, norm1_w, norm2_w, ssm_w_in, ssm_conv_w, ssm_conv_b, ssm_dt_bias, ssm_a_log, ssm_d, ssm_norm_w, ssm_w_out, mlstm_w_in, mlstm_conv_w, mlstm_conv_b, mlstm_gate_b, mlstm_norm_w, mlstm_w_out, ffn_w_up, ffn_conv_w, ffn_conv_b, ffn_w_down, final_norm_w):
    bsz, seq, d = x.shape
    n_ctx = ctx.shape[1]
    depth = ada_w.shape[0]
    assert depth == 2 and ssm_w_in.shape[0] == 1 and mlstm_w_in.shape[0] == 1
    lay = _Layout(bsz, n_ctx, seq)
    x_flat = x.reshape(bsz * seq, d)
    ctx_flat = ctx.reshape(bsz * n_ctx, d)

    n_rows = -(-(bsz + 1) // SUBLANES) * SUBLANES
    cvec = jnp.zeros((n_rows, d), F32).at[:bsz].set(c).at[bsz].set(c_ctx)
    mods = _ada(cvec, ada_w, ada_b).reshape(depth, n_rows, 6, 1, d)

    w_up = ffn_w_up.astype(BF16)
    w_down = ffn_w_down.astype(BF16)
    f = ffn_conv_b.shape[1]
    conv9 = ffn_conv_w.reshape(depth, 9, f)

    heads = ssm_d.shape[1]
    assert 2 * heads <= LANES
    inner = heads * SSM_HEAD_DIM
    bc_w = 2 * SSM_GROUPS * SSM_STATE
    conv_ch = inner + bc_w
    w_in = _pad_cols(ssm_w_in, 2 * inner + bc_w + LANES).astype(BF16)
    splits = [(0, inner), (inner, conv_ch), (inner + conv_ch, LANES)]
    z, xbc, dt = _project(
        lay, (ctx_flat, x_flat), norm1_w[0:1], mods, 0, 0, 1, w_in, splits, (BF16, BF16, F32),
        conv=(1, ssm_conv_w[0], ssm_conv_b[0:1]), name="in_proj_l0")
    dtb = _pad_cols(ssm_dt_bias[0].reshape(1, -1), LANES)
    alog = _pad_cols(ssm_a_log[0].reshape(1, -1), LANES)
    y_f, y_b = _ssd_scan(lay, xbc, dt, dtb, alog, heads=heads)
    d_row = jnp.repeat(ssm_d[0], SSM_HEAD_DIM).reshape(1, inner)
    xall = _ssm_out(lay, y_f, y_b, xbc, z, (ctx_flat, x_flat), mods, 0, d_row, ssm_norm_w[0:1],
                    ssm_w_out.astype(BF16))

    a, gl = _project(lay, (xall,), norm2_w[0:1], mods, 0, 3, 4, w_up, [(0, f), (f, f)], (F32, F32),
                     name="ffn_up_l0")
    xall = _ffn_down(lay, a, gl, xall, mods, 0, conv9[0], ffn_conv_b[0:1], w_down)

    qkw = mlstm_conv_b.shape[1]
    vw = mlstm_norm_w.shape[1]
    w_in = _pad_cols(mlstm_w_in, qkw + 2 * vw + LANES).astype(BF16)
    splits = [(0, qkw), (qkw, vw), (qkw + vw, vw), (qkw + 2 * vw, LANES)]
    qk, v, o_gate, gates = _project(
        lay, (xall,), norm1_w[1:2], mods, 1, 0, 1, w_in, splits, (BF16, BF16, BF16, F32),
        conv=(0, mlstm_conv_w[0], mlstm_conv_b[0:1]), name="in_proj_l1")
    gate_b = _pad_cols(mlstm_gate_b[0:1], LANES)
    h_f, h_b = _mlstm_scan(lay, qk, v, gates, gate_b)
    xlat = _mlstm_out(lay, h_f, h_b, o_gate, xall, mods, 1, mlstm_norm_w[0:1],
                      mlstm_w_out.astype(BF16))

    a, gl = _project(lay, (xlat,), norm2_w[1:2], mods, 1, 3, 4, w_up, [(0, f), (f, f)], (F32, F32),
                     skip=lay.ctx_tiles, name="ffn_up_l1")
    out = _ffn_down(lay, a, gl, xlat, mods, 1, conv9[1], ffn_conv_b[1:2], w_down,
                    skip=lay.ctx_tiles, final_w=final_norm_w.reshape(1, d))
    return out.reshape(bsz, seq, d)
```

```python
import functools

import jax
import jax.numpy as jnp
from jax import lax
from jax.experimental import pallas as pl
from jax.experimental.pallas import tpu as pltpu

F32 = jnp.float32
BF16 = jnp.bfloat16
HIGHEST = lax.Precision.HIGHEST

NORM_EPS = 1e-6
GRID_WIDTH = 64
SSM_HEAD_DIM = 64
SSM_GROUPS = 4
SSM_STATE = 128
MLSTM_HEADS = 4
GATE_CAP = 15.0
CHUNK = 128

LANES = 128
SUBLANES = 8
TOKEN_TILE = 512
COL_CHUNK = 512
FFN_COL_CHUNK = 256
FFN_SHIFT_BLOCK = 256
OUT_ROW_BLOCK = 256
OUT_COL_CHUNK = 256
VMEM_LIMIT = 56 * 1024 * 1024


class _Layout:
    def __init__(self, bsz, n_ctx, seq, tm=TOKEN_TILE):
        assert tm % n_ctx == 0 and (bsz * n_ctx) % tm == 0 and seq % tm == 0
        assert n_ctx % CHUNK == 0 and seq % CHUNK == 0 and seq % GRID_WIDTH == 0
        self.bsz, self.n_ctx, self.seq, self.tm = bsz, n_ctx, seq, tm
        self.ctx_rows, self.lat_rows = bsz * n_ctx, bsz * seq
        self.rows = self.ctx_rows + self.lat_rows
        self.ctx_tiles = self.ctx_rows // tm
        self.lat_tiles_per_batch = seq // tm
        self.lat_tiles = bsz * self.lat_tiles_per_batch
        self.tiles = self.ctx_tiles + self.lat_tiles
        self.ctx_chunks, self.lat_chunks = n_ctx // CHUNK, seq // CHUNK

    def mod_row(self, j):
        return jnp.where(j < self.ctx_tiles, self.bsz, (j - self.ctx_tiles) // self.lat_tiles_per_batch)

    def batch_pos(self, j):
        return (j - self.ctx_tiles) % self.lat_tiles_per_batch


def _params(n_axes=1):
    return pltpu.CompilerParams(
        dimension_semantics=("arbitrary",) * n_axes, vmem_limit_bytes=VMEM_LIMIT)


def _col_chunks(start, width, step=COL_CHUNK):
    out, c = [], start
    while c < start + width:
        w = min(step, start + width - c)
        out.append((c, w))
        c += w
    return out


def _silu(v):
    return v * jax.nn.sigmoid(v)


def _softplus(v):
    return jnp.maximum(v, 0.0) + jnp.log1p(jnp.exp(-jnp.abs(v)))


def _rms_scale(v):
    return v * lax.rsqrt(jnp.mean(v * v, axis=-1, keepdims=True) + NORM_EPS)


def _dot(a, b):
    return jnp.dot(a, b, preferred_element_type=F32)


def _dot_nt(a, b):
    return lax.dot_general(a, b, (((1,), (1,)), ((), ())), preferred_element_type=F32)


def _lockstep_inner(gens):
    results = [None] * len(gens)
    live = list(range(len(gens)))
    while live:
        for i in list(live):
            try:
                next(gens[i])
            except StopIteration as done:
                results[i] = done.value
                live.remove(i)
        yield
    return results


def _lockstep(gens):
    it = _lockstep_inner(gens)
    while True:
        try:
            next(it)
        except StopIteration as done:
            return done.value


def _full(a):
    return pl.BlockSpec(a.shape, lambda *_: (0,) * a.ndim)


def _layer_spec(a, layer):
    return pl.BlockSpec((None,) + a.shape[1:], lambda *_: (layer,) + (0,) * (a.ndim - 1),
                        pipeline_mode=pl.Buffered(1))


def _mod_spec(d, layer, k, row_fn):
    return pl.BlockSpec((None, None, None, 1, d), lambda j: (layer, row_fn(j), k, 0, 0))


def _rows_spec(rows, width, idx_fn):
    return pl.BlockSpec((rows, width), lambda j: (idx_fn(j), 0))


def _ada_kernel(c_ref, w_ref, b_ref, o_ref):
    s = _silu(c_ref[...])
    o_ref[...] = jnp.dot(s, w_ref[...], precision=HIGHEST, preferred_element_type=F32) + b_ref[...]


def _ada(cvec, ada_w, ada_b):
    depth, d, n = ada_w.shape
    rows = cvec.shape[0]
    tn = 512
    return pl.pallas_call(
        _ada_kernel,
        grid=(depth, n // tn),
        in_specs=[
            pl.BlockSpec((rows, d), lambda l, j: (0, 0)),
            pl.BlockSpec((None, d, tn), lambda l, j: (l, 0, j)),
            pl.BlockSpec((None, 1, tn), lambda l, j: (l, 0, j)),
        ],
        out_specs=pl.BlockSpec((None, rows, tn), lambda l, j: (l, 0, j)),
        out_shape=jax.ShapeDtypeStruct((depth, rows, n), F32),
        compiler_params=_params(2),
        name="ada_mod",
    )(cvec, ada_w, ada_b.reshape(depth, 1, n))


def _tok_specs(lay, arrs, rows, shift=0):
    per = lay.tm // rows
    if len(arrs) == 1:
        (a,) = arrs
        hi = a.shape[0] // rows - 1
        return [_rows_spec(rows, a.shape[1], lambda j: jnp.clip(j * per + shift, 0, hi))]
    ctx, lat = arrs
    hi_c = ctx.shape[0] // rows - 1
    hi_l = lat.shape[0] // rows - 1
    return [_rows_spec(rows, ctx.shape[1], lambda j: jnp.clip(j * per + shift, 0, hi_c)),
            _rows_spec(rows, lat.shape[1],
                       lambda j: jnp.clip((j - lay.ctx_tiles) * per + shift, 0, hi_l))]


def _tok_read(refs, is_ctx):
    if len(refs) == 1:
        return refs[0][...]
    return jnp.where(is_ctx, refs[0][...], refs[1][...])


def _proj_kernel(*refs, lay, n_src, splits, conv_idx, skip):
    refs = list(refs)
    x_refs = [refs.pop(0) for _ in range(n_src)]
    if conv_idx is not None:
        xp_refs = [refs.pop(0) for _ in range(n_src)]
        xn_refs = [refs.pop(0) for _ in range(n_src)]
    nw_ref, sh_ref, sc_ref, w_ref = refs[:4]
    refs = refs[4:]
    if conv_idx is not None:
        cw_ref, cb_ref = refs[:2]
        refs = refs[2:]
    outs = refs
    j = pl.program_id(0) + skip
    is_ctx = j < lay.ctx_tiles
    tm = lay.tm

    def prep(v):
        h = _rms_scale(v) * nw_ref[...]
        return h * (1.0 + sc_ref[...]) + sh_ref[...]

    h_main = prep(_tok_read(x_refs, is_ctx))
    hb = h_main.astype(BF16)
    if conv_idx is not None:
        pos = lay.batch_pos(j)
        is_first = jnp.logical_or(is_ctx, pos == 0)
        is_last = jnp.logical_or(is_ctx, pos == lay.lat_tiles_per_batch - 1)
        hp = jnp.where(is_first, 0.0, prep(_tok_read(xp_refs, is_ctx)))
        hn = jnp.where(is_last, 0.0, prep(_tok_read(xn_refs, is_ctx)))
        hext = jnp.concatenate([hp, h_main, hn], axis=0).astype(BF16)
        te = tm + 2 * SUBLANES
        seam_rows = lax.broadcasted_iota(jnp.int32, (te, 1), 0) - SUBLANES
        keep_up = jnp.where(jnp.logical_and(is_ctx, seam_rows % lay.n_ctx == 0), 0.0, 1.0)
        keep_dn = jnp.where(jnp.logical_and(is_ctx, seam_rows % lay.n_ctx == lay.n_ctx - 1), 0.0, 1.0)
    for oi, (s0, sw) in enumerate(splits):
        for (c0, cw) in _col_chunks(s0, sw):
            if oi == conv_idx:
                r = _dot(hext, w_ref[:, c0:c0 + cw])
                k0 = c0 - s0

                def conv(up, dn):
                    y = (up * cw_ref[0:1, k0:k0 + cw] + r * cw_ref[1:2, k0:k0 + cw]
                         + dn * cw_ref[2:3, k0:k0 + cw] + cb_ref[:, k0:k0 + cw])
                    return _silu(y[SUBLANES:SUBLANES + tm, :])

                up, dn = pltpu.roll(r, 1, 0), pltpu.roll(r, te - 1, 0)
                r = conv(up * keep_up, dn * keep_dn)
            else:
                r = _dot(hb, w_ref[:, c0:c0 + cw])
            outs[oi][:, c0 - s0:c0 - s0 + cw] = r.astype(outs[oi].dtype)


def _project(lay, srcs, norm_w, mods, layer, k_shift, k_scale, w, splits, out_dtypes, *,
             conv=None, skip=0, name):
    d = srcs[0].shape[1]
    tm = lay.tm
    n_tiles = lay.tiles - skip
    row_fn = lambda j: lay.mod_row(j + skip)
    in_specs = _tok_specs(lay, srcs, tm)
    args = list(srcs)
    conv_idx = None
    if conv is not None:
        assert skip == 0
        conv_idx, conv_w, conv_b = conv
        in_specs += _tok_specs(lay, srcs, SUBLANES, shift=-1)
        in_specs += _tok_specs(lay, srcs, SUBLANES, shift=tm // SUBLANES)
        args += list(srcs) * 2
    in_specs += [_full(norm_w), _mod_spec(d, layer, k_shift, row_fn),
                 _mod_spec(d, layer, k_scale, row_fn), _layer_spec(w, layer if w.shape[0] > 1 else 0)]
    args += [norm_w, mods, mods, w]
    if conv is not None:
        in_specs += [_full(conv_w), _full(conv_b)]
        args += [conv_w, conv_b]
    out_specs = [_rows_spec(tm, sw, lambda j: j) for (_, sw) in splits]
    out_shape = [jax.ShapeDtypeStruct((n_tiles * tm, sw), dt) for (_, sw), dt in zip(splits, out_dtypes)]
    kern = functools.partial(_proj_kernel, lay=lay, n_src=len(srcs), splits=tuple(splits),
                             conv_idx=conv_idx, skip=skip)
    return pl.pallas_call(
        kern, grid=(n_tiles,), in_specs=in_specs, out_specs=out_specs, out_shape=out_shape,
        compiler_params=_params(), name=name,
    )(*args)


SSD_TABLES = ("w", "w_t", "dts", "dte", "ew", "etot")


def _ssd_prep(dt_ref, dtb_ref, alog_ref, *, reverse):
    q = CHUNK
    r_i = lax.broadcasted_iota(jnp.int32, (q, q), 0)
    c_i = lax.broadcasted_iota(jnp.int32, (q, q), 1)
    tri = (c_i >= r_i) if reverse else (c_i <= r_i)
    dts_t = _softplus(dt_ref[...] + dtb_ref[...])
    la_t = dts_t * (-jnp.exp(alog_ref[...]))
    yield
    w_t = jnp.dot(tri.astype(F32), la_t, precision=HIGHEST, preferred_element_type=F32)
    yield
    w = w_t.T
    dts = dts_t.T
    yield
    tot = w[:, 0:1] if reverse else w[:, q - 1:q]
    dte = jnp.exp(tot - w)
    ew = jnp.exp(w)
    etot = jnp.broadcast_to(jnp.exp(tot), (LANES, SSM_STATE))
    return dict(w=w, w_t=w_t, dts=dts, dte=dte, ew=ew, etot=etot)


def _ssd_mask(reverse):
    r_i = lax.broadcasted_iota(jnp.int32, (CHUNK, CHUNK), 0)
    c_i = lax.broadcasted_iota(jnp.int32, (CHUNK, CHUNK), 1)
    return (r_i >= c_i) if reverse else (r_i <= c_i)


def _ssd_group(p, xbc_ref, st_ref, g, *, reverse, heads):
    hd, ng, ns = SSM_HEAD_DIM, SSM_GROUPS, SSM_STATE
    hpg = heads // ng
    inner = heads * hd
    dir_off = heads if reverse else 0
    w, w_t = p["w"], p["w_t"]
    bb = xbc_ref[:, inner + g * ns: inner + (g + 1) * ns]
    cb = xbc_ref[:, inner + ng * ns + g * ns: inner + ng * ns + (g + 1) * ns]
    cbt = _dot_nt(bb, cb)
    st_g = st_ref[g]
    yint = _dot_nt(st_g.astype(BF16), cb)
    y_parts, xdec_parts = [], []
    for pr in range(hpg // 2):
        col = (g * hpg + 2 * pr) * hd
        blk_t = xbc_ref[:, col:col + 2 * hd].astype(F32).T
        for half in range(2):
            r = 2 * pr + half
            hh = dir_off + g * hpg + r
            xd = blk_t[half * hd:(half + 1) * hd, :] * p["dts"][hh:hh + 1, :]
            expo = jnp.where(p["mask_sl"], w[hh:hh + 1, :] - w_t[:, hh:hh + 1], -jnp.inf)
            mt = (cbt * jnp.exp(expo)).astype(BF16)
            y_h = (_dot(xd.astype(BF16), mt)
                   + yint[r * hd:(r + 1) * hd, :] * p["ew"][hh:hh + 1, :])
            y_parts.append(y_h)
            xdec_parts.append((xd * p["dte"][hh:hh + 1, :]).astype(BF16))
    xdec = jnp.concatenate(xdec_parts, axis=0)
    sloc = _dot(xdec, bb)
    new_state = [st_g[r * hd:(r + 1) * hd, :] * p["etot"][dir_off + g * hpg + r:dir_off + g * hpg + r + 1, :]
                 + sloc[r * hd:(r + 1) * hd, :] for r in range(hpg)]
    y_blocks = [jnp.concatenate(y_parts[2 * pr:2 * pr + 2], axis=0).T for pr in range(hpg // 2)]
    return y_blocks, new_state


def _ssd_kernel(xf_ref, dtf_ref, dtf_next_ref, xb_ref, dtb2_ref, dtb_next_ref, dtb_ref, alog_ref,
                yf_ref, yb_ref, stf_ref, stb_ref, tabf_ref, tabb_ref, *, heads):
    def park(tab_ref, tables):
        for k, name in enumerate(SSD_TABLES):
            tab_ref[k] = tables[name]

    @pl.when(pl.program_id(1) == 0)
    def _():
        stf_ref[...] = jnp.zeros_like(stf_ref)
        stb_ref[...] = jnp.zeros_like(stb_ref)
        first = _lockstep([_ssd_prep(dtf_ref, dtb_ref, alog_ref, reverse=False),
                           _ssd_prep(dtb2_ref, dtb_ref, alog_ref, reverse=True)])
        park(tabf_ref, first[0])
        park(tabb_ref, first[1])

    hd = SSM_HEAD_DIM
    hpg = heads // SSM_GROUPS
    chains = []
    for (tab_ref, x_ref, st_ref, y_ref, rev) in ((tabf_ref, xf_ref, stf_ref, yf_ref, False),
                                                 (tabb_ref, xb_ref, stb_ref, yb_ref, True)):
        p = {name: tab_ref[k] for k, name in enumerate(SSD_TABLES)}
        p["mask_sl"] = _ssd_mask(rev)
        chains.append((p, x_ref, st_ref, y_ref, rev))
    upcoming = _lockstep_inner([_ssd_prep(dtf_next_ref, dtb_ref, alog_ref, reverse=False),
                                _ssd_prep(dtb_next_ref, dtb_ref, alog_ref, reverse=True)])
    nxt = None
    for g in range(SSM_GROUPS):
        results = [_ssd_group(p, x_ref, st_ref, g, reverse=rev, heads=heads)
                   for (p, x_ref, st_ref, _, rev) in chains]
        if nxt is None:
            try:
                next(upcoming)
            except StopIteration as done:
                nxt = done.value
        for (_, _, st_ref, y_ref, _), (y_blocks, new_state) in zip(chains, results):
            for r, s_new in enumerate(new_state):
                st_ref[g, r * hd:(r + 1) * hd, :] = s_new
            for pr, blk in enumerate(y_blocks):
                col = (g * hpg + 2 * pr) * hd
                y_ref[:, col:col + 2 * hd] = blk.astype(y_ref.dtype)
    while nxt is None:
        try:
            next(upcoming)
        except StopIteration as done:
            nxt = done.value
    park(tabf_ref, nxt[0])
    park(tabb_ref, nxt[1])


def _chunk_block(lay, reverse):
    ncc, nlc = lay.ctx_chunks, lay.lat_chunks

    def block(b, i):
        in_ctx = i < ncc
        c_ctx = (ncc - 1 - i) if reverse else i
        c_lat = (nlc - 1 - (i - ncc)) if reverse else (i - ncc)
        return jnp.where(in_ctx, b * ncc + c_ctx, lay.bsz * ncc + b * nlc + c_lat)

    return block


def _ssd_scan(lay, xbc, dt, dtb, alog, *, heads):
    cch = xbc.shape[1]
    inner = heads * SSM_HEAD_DIM
    assert CHUNK == LANES == SSM_STATE
    fwd, bwd = _chunk_block(lay, False), _chunk_block(lay, True)
    n_steps = lay.ctx_chunks + lay.lat_chunks
    after = lambda order: (lambda b, i: order(b, jnp.minimum(i + 1, n_steps - 1)))
    blk = lambda width, order: pl.BlockSpec((CHUNK, width), lambda b, i: (order(b, i), 0))
    state = pltpu.VMEM((SSM_GROUPS, inner // SSM_GROUPS, SSM_STATE), F32)
    tables = pltpu.VMEM((len(SSD_TABLES), LANES, CHUNK), F32)
    return pl.pallas_call(
        functools.partial(_ssd_kernel, heads=heads),
        grid=(lay.bsz, n_steps),
        in_specs=[blk(cch, fwd), blk(LANES, fwd), blk(LANES, after(fwd)),
                  blk(cch, bwd), blk(LANES, bwd), blk(LANES, after(bwd)),
                  pl.BlockSpec((1, LANES), lambda b, i: (0, 0)),
                  pl.BlockSpec((1, LANES), lambda b, i: (0, 0))],
        out_specs=[blk(inner, fwd), blk(inner, bwd)],
        out_shape=[jax.ShapeDtypeStruct((lay.rows, inner), BF16)] * 2,
        scratch_shapes=[state, state, tables, tables],
        compiler_params=_params(2),
        name="ssd_scan",
    )(xbc, dt, dt, xbc, dt, dt, dtb, alog)


def _ssm_out_kernel(*refs, lay):
    yf_ref, yb_ref, xs_ref, z_ref, xc_ref, xl_ref, g_ref, d_ref, nw_ref, w_ref, o_ref = refs
    is_ctx = pl.program_id(0) < lay.ctx_tiles
    inner = z_ref.shape[1]
    acc = None
    sumsq = None
    for (c0, cw) in _col_chunks(0, inner, OUT_COL_CHUNK):
        cs = slice(c0, c0 + cw)
        z = z_ref[:, cs].astype(F32)
        y = (yf_ref[:, cs].astype(F32) + yb_ref[:, cs].astype(F32)
             + d_ref[:, cs] * xs_ref[:, cs].astype(F32)) * _silu(z)
        sq = y * y
        part_sq = functools.reduce(lambda u, v: u + v, [sq[:, k:k + LANES] for k in range(0, cw, LANES)])
        sumsq = part_sq if sumsq is None else sumsq + part_sq
        part = _dot((y * nw_ref[:, cs]).astype(BF16), w_ref[cs, :])
        acc = part if acc is None else acc + part
    scale = lax.rsqrt(jnp.sum(sumsq, axis=-1, keepdims=True) * (1.0 / inner) + NORM_EPS)
    resid = jnp.where(is_ctx, xc_ref[...], xl_ref[...])
    o_ref[...] = resid + g_ref[...] * (scale * acc)


def _ssm_out(lay, yf, yb, xbc, z, srcs, mods, layer, d_row, norm_w, w_out):
    inner = yf.shape[1]
    d = srcs[0].shape[1]
    tm = lay.tm
    tok = lambda c: _rows_spec(tm, c, lambda j: j)
    return pl.pallas_call(
        functools.partial(_ssm_out_kernel, lay=lay), grid=(lay.tiles,),
        in_specs=[tok(inner), tok(inner), tok(inner), tok(inner)]
        + _tok_specs(lay, srcs, tm)
        + [_mod_spec(d, layer, 2, lay.mod_row), _full(d_row), _full(norm_w), _layer_spec(w_out, 0)],
        out_specs=tok(d),
        out_shape=jax.ShapeDtypeStruct((lay.rows, d), F32),
        compiler_params=_params(), name="ssm_out",
    )(yf, yb, xbc, z, *srcs, mods, d_row, norm_w, w_out)


MLSTM_TABLES = ("wm_t", "rowterm", "inter_t", "floor_t", "e_r")
MLSTM_ROWS = ("a_prev", "a_loc", "m_new")


def _mlstm_keep(reverse):
    r_i = lax.broadcasted_iota(jnp.int32, (CHUNK, CHUNK), 0)
    c_i = lax.broadcasted_iota(jnp.int32, (CHUNK, CHUNK), 1)
    return (c_i >= r_i) if reverse else (c_i <= r_i)


def _mlstm_prep(gt_ref, gb_ref, m_prev, *, reverse):
    q = CHUNK
    nh = MLSTM_HEADS
    d = 1 if reverse else 0

    r_i = lax.broadcasted_iota(jnp.int32, (q, q), 0)
    c_i = lax.broadcasted_iota(jnp.int32, (q, q), 1)
    keep = _mlstm_keep(reverse)

    gt = GATE_CAP * jnp.tanh((gt_ref[...] + gb_ref[...]) / GATE_CAP)
    lane_ok = c_i < nh
    ig_sh = (LANES - nh * d) % LANES
    fg_sh = LANES - (2 * nh + nh * d)
    ig_t = gt if ig_sh == 0 else pltpu.roll(gt, ig_sh, 1)
    ig_t = jnp.where(lane_ok, ig_t, 0.0)
    fr_t = pltpu.roll(gt, fg_sh, 1)
    lf_t = jnp.where(lane_ok, -_softplus(-fr_t), 0.0)
    yield
    wv_t = jnp.dot(keep.astype(F32), lf_t, precision=HIGHEST, preferred_element_type=F32)
    ftot = jnp.sum(lf_t, axis=0, keepdims=True)
    yield
    wend_t = ftot - wv_t + ig_t
    mloc = jnp.max(wend_t, axis=0, keepdims=True)
    e_t = jnp.exp(wend_t - mloc)
    m_new = jnp.maximum(ftot + m_prev, mloc)
    a_prev = jnp.exp(ftot + m_prev - m_new)
    a_loc = jnp.exp(mloc - m_new)
    run = ig_t - wv_t
    sh = 1
    while sh < q:
        if reverse:
            moved, ok = pltpu.roll(run, q - sh, 0), r_i < q - sh
        else:
            moved, ok = pltpu.roll(run, sh, 0), r_i >= sh
        run = jnp.maximum(run, jnp.where(ok, moved, -jnp.inf))
        sh *= 2
    yield
    gcol_t = wv_t + m_prev
    mcomb_t = jnp.maximum(gcol_t, wv_t + run)
    wv = wv_t.T
    return dict(wm_t=wv_t - mcomb_t, rowterm=ig_t.T - wv, inter_t=jnp.exp(gcol_t - mcomb_t),
                floor_t=jnp.exp(-mcomb_t), e_r=e_t.T, a_prev=a_prev, a_loc=a_loc, m_new=m_new)


def _mlstm_head(p, qk_ref, v_ref, c_ref, h):
    q = CHUNK
    nh = MLSTM_HEADS
    dk = qk_ref.shape[-1] // (2 * nh)
    dv = v_ref.shape[-1] // nh
    qb = qk_ref[:, h * dk:(h + 1) * dk]
    kt = (qk_ref[:, nh * dk + h * dk: nh * dk + (h + 1) * dk].astype(F32) * (dk ** -0.5)).T
    vaug = jnp.concatenate([v_ref[:, h * dv:(h + 1) * dv], jnp.ones((q, LANES), BF16)], axis=1)
    caug = c_ref[h]
    lanes = lambda col: jnp.broadcast_to(col, (q, LANES))
    pm = jnp.exp(jnp.where(p["keep"], p["wm_t"][:, h:h + 1] + p["rowterm"][h:h + 1, :], -jnp.inf))
    inter = lanes(p["inter_t"][:, h:h + 1])
    floor = lanes(p["floor_t"][:, h:h + 1])
    yield
    s = (_dot(qb, kt.astype(BF16)) * pm).astype(BF16)
    qc = _dot(qb, caug.astype(BF16))
    yield
    sv = _dot(s, vaug)
    ap = p["a_prev"][:, h:h + 1]
    al = p["a_loc"][:, h:h + 1]
    c_new = ap * caug + al * _dot((kt * p["e_r"][h:h + 1, :]).astype(BF16), vaug)
    yield
    blocks = [sv[:, c:c + LANES] + inter * qc[:, c:c + LANES] for c in range(0, dv + LANES, LANES)]
    rden = 1.0 / jnp.maximum(jnp.abs(blocks[-1]), floor)
    h_out = jnp.concatenate([b * rden for b in blocks[:-1]], axis=1)
    return h_out, c_new


def _mlstm_kernel(qkf_ref, vf_ref, gtf_ref, gtf_next_ref, qkb_ref, vb_ref, gtb_ref, gtb_next_ref,
                  gb_ref, hf_ref, hb_ref, cf_ref, tabf_ref, rowf_ref, cb_ref, tabb_ref, rowb_ref):
    def park(tab_ref, row_ref, tables):
        for k, name in enumerate(MLSTM_TABLES):
            tab_ref[k] = tables[name]
        for k, name in enumerate(MLSTM_ROWS):
            row_ref[k] = tables[name]

    @pl.when(pl.program_id(1) == 0)
    def _():
        cf_ref[...] = jnp.zeros_like(cf_ref)
        cb_ref[...] = jnp.zeros_like(cb_ref)
        zero = jnp.zeros((1, LANES), F32)
        first = _lockstep([_mlstm_prep(gtf_ref, gb_ref, zero, reverse=False),
                           _mlstm_prep(gtb_ref, gb_ref, zero, reverse=True)])
        park(tabf_ref, rowf_ref, first[0])
        park(tabb_ref, rowb_ref, first[1])

    dv = vf_ref.shape[-1] // MLSTM_HEADS
    chains = []
    for (tab_ref, row_ref, qk_ref, v_ref, c_ref, h_ref, rev) in (
            (tabf_ref, rowf_ref, qkf_ref, vf_ref, cf_ref, hf_ref, False),
            (tabb_ref, rowb_ref, qkb_ref, vb_ref, cb_ref, hb_ref, True)):
        p = {name: tab_ref[k] for k, name in enumerate(MLSTM_TABLES)}
        p.update({name: row_ref[k] for k, name in enumerate(MLSTM_ROWS)})
        p["keep"] = _mlstm_keep(rev)
        chains.append((p, qk_ref, v_ref, c_ref, h_ref))
    units = [(p, qk_ref, v_ref, c_ref, h_ref, h)
             for (p, qk_ref, v_ref, c_ref, h_ref) in chains for h in range(MLSTM_HEADS)]
    results = _lockstep(
        [_mlstm_head(p, qk_ref, v_ref, c_ref, h) for (p, qk_ref, v_ref, c_ref, _, h) in units]
        + [_mlstm_prep(gtf_next_ref, gb_ref, chains[0][0]["m_new"], reverse=False),
           _mlstm_prep(gtb_next_ref, gb_ref, chains[1][0]["m_new"], reverse=True)])
    for (_, _, _, c_ref, h_ref, h), (h_out, c_new) in zip(units, results):
        h_ref[:, h * dv:(h + 1) * dv] = h_out.astype(h_ref.dtype)
        c_ref[h] = c_new
    park(tabf_ref, rowf_ref, results[-2])
    park(tabb_ref, rowb_ref, results[-1])


def _mlstm_scan(lay, qk, v, gates, gate_b):
    qkw, vw = qk.shape[1], v.shape[1]
    nh = MLSTM_HEADS
    dk, dv = qkw // (2 * nh), vw // nh
    ncc, nlc = lay.ctx_chunks, lay.lat_chunks
    fwd, bwd = _chunk_block(lay, False), _chunk_block(lay, True)
    out_f = lambda b, i: b * nlc + jnp.maximum(i - ncc, 0)
    out_b = lambda b, i: b * nlc + nlc - 1 - jnp.maximum(i - ncc, 0)
    assert CHUNK == LANES
    n_steps = ncc + nlc
    after = lambda order: (lambda b, i: order(b, jnp.minimum(i + 1, n_steps - 1)))
    blk = lambda width, order: pl.BlockSpec((CHUNK, width), lambda b, i: (order(b, i), 0))
    ins = lambda order: [blk(qkw, order), blk(vw, order), blk(LANES, order), blk(LANES, after(order))]
    state = [pltpu.VMEM((nh, dk, dv + LANES), F32),
             pltpu.VMEM((len(MLSTM_TABLES), CHUNK, LANES), F32),
             pltpu.VMEM((len(MLSTM_ROWS), 1, LANES), F32)]
    return pl.pallas_call(
        _mlstm_kernel,
        grid=(lay.bsz, n_steps),
        in_specs=ins(fwd) + ins(bwd) + [pl.BlockSpec((1, LANES), lambda b, i: (0, 0))],
        out_specs=[blk(vw, out_f), blk(vw, out_b)],
        out_shape=[jax.ShapeDtypeStruct((lay.lat_rows, vw), BF16)] * 2,
        scratch_shapes=state + state,
        compiler_params=_params(2),
        name="mlstm_scan",
    )(qk, v, gates, gates, qk, v, gates, gates, gate_b)


def _mlstm_out_kernel(hf_ref, hb_ref, o_ref_in, x_ref, g_ref, nw_ref, w_ref, out_ref):
    nh = MLSTM_HEADS
    dv = hf_ref.shape[-1] // nh
    for r0 in range(0, hf_ref.shape[0], OUT_ROW_BLOCK):
        rows = slice(r0, r0 + OUT_ROW_BLOCK)
        parts = []
        for h in range(nh):
            hh = (hf_ref[rows, h * dv:(h + 1) * dv].astype(F32)
                  + hb_ref[rows, h * dv:(h + 1) * dv].astype(F32))
            parts.append(_rms_scale(hh))
        hn = jnp.concatenate(parts, axis=1)
        y = (hn * nw_ref[...] * jax.nn.sigmoid(o_ref_in[rows, :].astype(F32))).astype(BF16)
        out_ref[rows, :] = x_ref[rows, :] + g_ref[...] * _dot(y, w_ref[...])


def _mlstm_out(lay, hf, hb, o_gate, xall, mods, layer, norm_w, w_out):
    vw = hf.shape[1]
    d = xall.shape[1]
    tm = lay.tm
    lat = lambda c: _rows_spec(tm, c, lambda j: j)
    allt = lambda c: _rows_spec(tm, c, lambda j: j + lay.ctx_tiles)
    return pl.pallas_call(
        _mlstm_out_kernel, grid=(lay.lat_tiles,),
        in_specs=[lat(vw), lat(vw), allt(vw), allt(d),
                  _mod_spec(d, layer, 2, lambda j: j // lay.lat_tiles_per_batch),
                  _full(norm_w), _layer_spec(w_out, 0)],
        out_specs=lat(d),
        out_shape=jax.ShapeDtypeStruct((lay.lat_rows, d), F32),
        compiler_params=_params(), name="mlstm_out",
    )(hf, hb, o_gate, xall, mods, norm_w, w_out)


def _ffn_down_kernel(*refs, lay, skip, final_norm):
    if final_norm:
        a_ref, ap_ref, an_ref, gl_ref, x_ref, g_ref, cw_ref, cb_ref, w_ref, fw_ref, o_ref = refs
    else:
        a_ref, ap_ref, an_ref, gl_ref, x_ref, g_ref, cw_ref, cb_ref, w_ref, o_ref = refs
    tm, f = a_ref.shape
    gw = GRID_WIDTH
    j = pl.program_id(0) + skip
    is_ctx = j < lay.ctx_tiles
    pos = lay.batch_pos(j)
    has_prev = jnp.logical_and(jnp.logical_not(is_ctx), pos > 0)
    has_next = jnp.logical_and(jnp.logical_not(is_ctx), pos < lay.lat_tiles_per_batch - 1)

    cw = FFN_COL_CHUNK
    vert = jnp.where(is_ctx, 0.0, 1.0)
    sb = FFN_SHIFT_BLOCK
    ri = lax.broadcasted_iota(jnp.int32, (sb, sb), 0)
    ci = lax.broadcasted_iota(jnp.int32, (sb, sb), 1)
    colpos = jnp.where(is_ctx, ri % lay.n_ctx, ri & (gw - 1))
    last_col = jnp.where(is_ctx, lay.n_ctx - 1, gw - 1)
    take_left = jnp.where(colpos != 0, jnp.where(ci == ri - 1, 1.0, 0.0), 0.0)
    take_right = jnp.where(colpos != last_col, jnp.where(ci == ri + 1, 1.0, 0.0), 0.0)
    shift = jnp.concatenate([take_left, take_right], axis=1).astype(BF16)

    def taps(c0):
        cs = slice(c0, c0 + cw)
        up = jnp.where(has_prev, ap_ref[:, cs], 0.0)
        dn = jnp.where(has_next, an_ref[:, cs], 0.0)
        slabs = [jnp.concatenate([up, a_ref[0:tm - gw, cs]], axis=0), a_ref[:, cs],
                 jnp.concatenate([a_ref[gw:tm, cs], dn], axis=0)]

        def vsum(dj):
            k = lambda di: 3 * (di + 1) + (dj + 1)
            return (slabs[0] * (cw_ref[k(-1):k(-1) + 1, cs] * vert)
                    + slabs[1] * cw_ref[k(0):k(0) + 1, cs]
                    + slabs[2] * (cw_ref[k(1):k(1) + 1, cs] * vert))

        left, right = vsum(-1).astype(BF16), vsum(1).astype(BF16)
        sides = jnp.concatenate(
            [_dot(shift, jnp.concatenate([left[r0:r0 + sb], right[r0:r0 + sb]], axis=0))
             for r0 in range(0, tm, sb)], axis=0)
        return vsum(0) + cb_ref[:, cs], sides

    def finish(c0, centre, sides, acc_out):
        cs = slice(c0, c0 + cw)
        mid = (_silu(centre + sides) * gl_ref[:, cs].astype(F32)).astype(BF16)
        part = _dot(mid, w_ref[cs, :])
        return part if acc_out is None else acc_out + part

    acc_out = None
    pending = None
    for c0 in range(0, f, cw):
        cur = (c0,) + taps(c0)
        if pending is not None:
            acc_out = finish(*pending, acc_out)
        pending = cur
    acc_out = finish(*pending, acc_out)
    out = x_ref[...] + g_ref[...] * acc_out
    if final_norm:
        out = _rms_scale(out) * fw_ref[...]
    o_ref[...] = out


def _ffn_down(lay, a, gl, xres, mods, layer, conv_w9, conv_b, w_down, *, skip=0, final_w=None):
    tt, f = a.shape
    d = xres.shape[1]
    tm = lay.tm
    gw = GRID_WIDTH
    assert f % FFN_COL_CHUNK == 0 and tm % FFN_SHIFT_BLOCK == 0
    assert FFN_SHIFT_BLOCK % gw == 0 and FFN_SHIFT_BLOCK % lay.n_ctx == 0
    n_tiles = tt // tm
    hb = tm // gw
    tok = lambda c: _rows_spec(tm, c, lambda j: j)
    in_specs = [
        tok(f),
        _rows_spec(gw, f, lambda j: jnp.maximum(j * hb - 1, 0)),
        _rows_spec(gw, f, lambda j: jnp.minimum((j + 1) * hb, tt // gw - 1)),
        tok(f), tok(d), _mod_spec(d, layer, 5, lambda j: lay.mod_row(j + skip)),
        _full(conv_w9), _full(conv_b), _layer_spec(w_down, layer)]
    args = [a, a, a, gl, xres, mods, conv_w9, conv_b, w_down]
    if final_w is not None:
        in_specs.append(_full(final_w))
        args.append(final_w)
    kern = functools.partial(_ffn_down_kernel, lay=lay, skip=skip, final_norm=final_w is not None)
    return pl.pallas_call(
        kern, grid=(n_tiles,), in_specs=in_specs, out_specs=tok(d),
        out_shape=jax.ShapeDtypeStruct((tt, d), F32),
        compiler_params=_params(), name=f"ffn_down_l{layer}",
    )(*args)


def _pad_cols(a, n):
    return jnp.pad(a, [(0, 0)] * (a.ndim - 1) + [(0, n - a.shape[-1])])


def kernel(x, c, ctx, c_ctx, ada_w, ada_b, norm1_w, norm2_w, ssm_w_in, ssm_conv_w, ssm_conv_b, ssm_dt_bias, ssm_a_log, ssm_d, ssm_norm_w, ssm_w_out, mlstm_w_in, mlstm_conv_w, mlstm_conv_b, mlstm_gate_b, mlstm_norm_w, mlstm_w_out, ffn_w_up, ffn_conv_w, ffn_conv_b, ffn_w_down, final_norm_w):
    bsz, seq, d = x.shape
    n_ctx = ctx.shape[1]
    depth = ada_w.shape[0]
    assert depth == 2 and ssm_w_in.shape[0] == 1 and mlstm_w_in.shape[0] == 1
    lay = _Layout(bsz, n_ctx, seq)
    x_flat = x.reshape(bsz * seq, d)
    ctx_flat = ctx.reshape(bsz * n_ctx, d)

    n_rows = -(-(bsz + 1) // SUBLANES) * SUBLANES
    cvec = jnp.zeros((n_rows, d), F32).at[:bsz].set(c).at[bsz].set(c_ctx)
    mods = _ada(cvec, ada_w, ada_b).reshape(depth, n_rows, 6, 1, d)

    w_up = ffn_w_up.astype(BF16)
    w_down = ffn_w_down.astype(BF16)
    f = ffn_conv_b.shape[1]
    conv9 = ffn_conv_w.reshape(depth, 9, f)

    heads = ssm_d.shape[1]
    assert 2 * heads <= LANES
    inner = heads * SSM_HEAD_DIM
    bc_w = 2 * SSM_GROUPS * SSM_STATE
    conv_ch = inner + bc_w
    w_in = _pad_cols(ssm_w_in, 2 * inner + bc_w + LANES).astype(BF16)
    splits = [(0, inner), (inner, conv_ch), (inner + conv_ch, LANES)]
    z, xbc, dt = _project(
        lay, (ctx_flat, x_flat), norm1_w[0:1], mods, 0, 0, 1, w_in, splits, (BF16, BF16, F32),
        conv=(1, ssm_conv_w[0], ssm_conv_b[0:1]), name="in_proj_l0")
    dtb = _pad_cols(ssm_dt_bias[0].reshape(1, -1), LANES)
    alog = _pad_cols(ssm_a_log[0].reshape(1, -1), LANES)
    y_f, y_b = _ssd_scan(lay, xbc, dt, dtb, alog, heads=heads)
    d_row = jnp.repeat(ssm_d[0], SSM_HEAD_DIM).reshape(1, inner)
    xall = _ssm_out(lay, y_f, y_b, xbc, z, (ctx_flat, x_flat), mods, 0, d_row, ssm_norm_w[0:1],
                    ssm_w_out.astype(BF16))

    a, gl = _project(lay, (xall,), norm2_w[0:1], mods, 0, 3, 4, w_up, [(0, f), (f, f)], (F32, BF16),
                     name="ffn_up_l0")
    xall = _ffn_down(lay, a, gl, xall, mods, 0, conv9[0], ffn_conv_b[0:1], w_down)

    qkw = mlstm_conv_b.shape[1]
    vw = mlstm_norm_w.shape[1]
    w_in = _pad_cols(mlstm_w_in, qkw + 2 * vw + LANES).astype(BF16)
    splits = [(0, qkw), (qkw, vw), (qkw + vw, vw), (qkw + 2 * vw, LANES)]
    qk, v, o_gate, gates = _project(
        lay, (xall,), norm1_w[1:2], mods, 1, 0, 1, w_in, splits, (BF16, BF16, BF16, F32),
        conv=(0, mlstm_conv_w[0], mlstm_conv_b[0:1]), name="in_proj_l1")
    gate_b = _pad_cols(mlstm_gate_b[0:1], LANES)
    h_f, h_b = _mlstm_scan(lay, qk, v, gates, gate_b)
    xlat = _mlstm_out(lay, h_f, h_b, o_gate, xall, mods, 1, mlstm_norm_w[0:1],
                      mlstm_w_out.astype(BF16))

    a, gl = _project(lay, (xlat,), norm2_w[1:2], mods, 1, 3, 4, w_up, [(0, f), (f, f)], (F32, BF16),
                     skip=lay.ctx_tiles, name="ffn_up_l1")
    out = _ffn_down(lay, a, gl, xlat, mods, 1, conv9[1], ffn_conv_b[1:2], w_down,
                    skip=lay.ctx_tiles, final_w=final_norm_w.reshape(1, d))
    return out.reshape(bsz, seq, d)
```

```python
import functools

import jax
import jax.numpy as jnp
from jax import lax
from jax.experimental import pallas as pl
from jax.experimental.pallas import tpu as pltpu

F32 = jnp.float32
BF16 = jnp.bfloat16
HIGHEST = lax.Precision.HIGHEST

NORM_EPS = 1e-6
GRID_WIDTH = 64
SSM_HEAD_DIM = 64
SSM_GROUPS = 4
SSM_STATE = 128
MLSTM_HEADS = 4
GATE_CAP = 15.0
CHUNK = 128

LANES = 128
SUBLANES = 8
TOKEN_TILE = 512
COL_CHUNK = 512
FFN_COL_CHUNK = 256
FFN_SHIFT_BLOCK = 256
OUT_ROW_BLOCK = 256
OUT_COL_CHUNK = 256
VMEM_LIMIT = 56 * 1024 * 1024


class _Layout:
    def __init__(self, bsz, n_ctx, seq, tm=TOKEN_TILE):
        assert tm % n_ctx == 0 and (bsz * n_ctx) % tm == 0 and seq % tm == 0
        assert n_ctx % CHUNK == 0 and seq % CHUNK == 0 and seq % GRID_WIDTH == 0
        self.bsz, self.n_ctx, self.seq, self.tm = bsz, n_ctx, seq, tm
        self.ctx_rows, self.lat_rows = bsz * n_ctx, bsz * seq
        self.rows = self.ctx_rows + self.lat_rows
        self.ctx_tiles = self.ctx_rows // tm
        self.lat_tiles_per_batch = seq // tm
        self.lat_tiles = bsz * self.lat_tiles_per_batch
        self.tiles = self.ctx_tiles + self.lat_tiles
        self.ctx_chunks, self.lat_chunks = n_ctx // CHUNK, seq // CHUNK

    def mod_row(self, j):
        return jnp.where(j < self.ctx_tiles, self.bsz, (j - self.ctx_tiles) // self.lat_tiles_per_batch)

    def batch_pos(self, j):
        return (j - self.ctx_tiles) % self.lat_tiles_per_batch


def _params(n_axes=1):
    return pltpu.CompilerParams(
        dimension_semantics=("arbitrary",) * n_axes, vmem_limit_bytes=VMEM_LIMIT)


def _col_chunks(start, width, step=COL_CHUNK):
    out, c = [], start
    while c < start + width:
        w = min(step, start + width - c)
        out.append((c, w))
        c += w
    return out


def _silu(v):
    return v * jax.nn.sigmoid(v)


def _softplus(v):
    return jnp.maximum(v, 0.0) + jnp.log1p(jnp.exp(-jnp.abs(v)))


def _rms_scale(v):
    return v * lax.rsqrt(jnp.mean(v * v, axis=-1, keepdims=True) + NORM_EPS)


def _dot(a, b):
    return jnp.dot(a, b, preferred_element_type=F32)


def _dot_nt(a, b):
    return lax.dot_general(a, b, (((1,), (1,)), ((), ())), preferred_element_type=F32)


def _lockstep_inner(gens):
    results = [None] * len(gens)
    live = list(range(len(gens)))
    while live:
        for i in list(live):
            try:
                next(gens[i])
            except StopIteration as done:
                results[i] = done.value
                live.remove(i)
        yield
    return results


def _lockstep(gens):
    it = _lockstep_inner(gens)
    while True:
        try:
            next(it)
        except StopIteration as done:
            return done.value


def _full(a):
    return pl.BlockSpec(a.shape, lambda *_: (0,) * a.ndim)


def _layer_spec(a, layer):
    return pl.BlockSpec((None,) + a.shape[1:], lambda *_: (layer,) + (0,) * (a.ndim - 1),
                        pipeline_mode=pl.Buffered(1))


def _mod_spec(d, layer, k, row_fn):
    return pl.BlockSpec((None, None, None, 1, d), lambda j: (layer, row_fn(j), k, 0, 0))


def _rows_spec(rows, width, idx_fn):
    return pl.BlockSpec((rows, width), lambda j: (idx_fn(j), 0))


def _ada_kernel(c_ref, w_ref, b_ref, o_ref):
    s = _silu(c_ref[...])
    o_ref[...] = jnp.dot(s, w_ref[...], precision=HIGHEST, preferred_element_type=F32) + b_ref[...]


def _ada(cvec, ada_w, ada_b):
    depth, d, n = ada_w.shape
    rows = cvec.shape[0]
    tn = 512
    return pl.pallas_call(
        _ada_kernel,
        grid=(depth, n // tn),
        in_specs=[
            pl.BlockSpec((rows, d), lambda l, j: (0, 0)),
            pl.BlockSpec((None, d, tn), lambda l, j: (l, 0, j)),
            pl.BlockSpec((None, 1, tn), lambda l, j: (l, 0, j)),
        ],
        out_specs=pl.BlockSpec((None, rows, tn), lambda l, j: (l, 0, j)),
        out_shape=jax.ShapeDtypeStruct((depth, rows, n), F32),
        compiler_params=_params(2),
        name="ada_mod",
    )(cvec, ada_w, ada_b.reshape(depth, 1, n))


def _tok_specs(lay, arrs, rows, shift=0):
    per = lay.tm // rows
    if len(arrs) == 1:
        (a,) = arrs
        hi = a.shape[0] // rows - 1
        return [_rows_spec(rows, a.shape[1], lambda j: jnp.clip(j * per + shift, 0, hi))]
    ctx, lat = arrs
    hi_c = ctx.shape[0] // rows - 1
    hi_l = lat.shape[0] // rows - 1
    return [_rows_spec(rows, ctx.shape[1], lambda j: jnp.clip(j * per + shift, 0, hi_c)),
            _rows_spec(rows, lat.shape[1],
                       lambda j: jnp.clip((j - lay.ctx_tiles) * per + shift, 0, hi_l))]


def _tok_read(refs, is_ctx):
    if len(refs) == 1:
        return refs[0][...]
    return jnp.where(is_ctx, refs[0][...], refs[1][...])


def _proj_kernel(*refs, lay, n_src, splits, conv_idx, skip):
    refs = list(refs)
    x_refs = [refs.pop(0) for _ in range(n_src)]
    if conv_idx is not None:
        xp_refs = [refs.pop(0) for _ in range(n_src)]
        xn_refs = [refs.pop(0) for _ in range(n_src)]
    nw_ref, sh_ref, sc_ref, w_ref = refs[:4]
    refs = refs[4:]
    if conv_idx is not None:
        cw_ref, cb_ref = refs[:2]
        refs = refs[2:]
    outs = refs
    j = pl.program_id(0) + skip
    is_ctx = j < lay.ctx_tiles
    tm = lay.tm

    def prep(v):
        h = _rms_scale(v) * nw_ref[...]
        return h * (1.0 + sc_ref[...]) + sh_ref[...]

    h_main = prep(_tok_read(x_refs, is_ctx))
    hb = h_main.astype(BF16)
    if conv_idx is not None:
        pos = lay.batch_pos(j)
        is_first = jnp.logical_or(is_ctx, pos == 0)
        is_last = jnp.logical_or(is_ctx, pos == lay.lat_tiles_per_batch - 1)
        hp = jnp.where(is_first, 0.0, prep(_tok_read(xp_refs, is_ctx)))
        hn = jnp.where(is_last, 0.0, prep(_tok_read(xn_refs, is_ctx)))
        hext = jnp.concatenate([hp, h_main, hn], axis=0).astype(BF16)
        te = tm + 2 * SUBLANES
        seam_rows = lax.broadcasted_iota(jnp.int32, (te, 1), 0) - SUBLANES
        keep_up = jnp.where(jnp.logical_and(is_ctx, seam_rows % lay.n_ctx == 0), 0.0, 1.0)
        keep_dn = jnp.where(jnp.logical_and(is_ctx, seam_rows % lay.n_ctx == lay.n_ctx - 1), 0.0, 1.0)
    for oi, (s0, sw) in enumerate(splits):
        for (c0, cw) in _col_chunks(s0, sw):
            if oi == conv_idx:
                r = _dot(hext, w_ref[:, c0:c0 + cw])
                k0 = c0 - s0

                def conv(up, dn):
                    y = (up * cw_ref[0:1, k0:k0 + cw] + r * cw_ref[1:2, k0:k0 + cw]
                         + dn * cw_ref[2:3, k0:k0 + cw] + cb_ref[:, k0:k0 + cw])
                    return _silu(y[SUBLANES:SUBLANES + tm, :])

                up, dn = pltpu.roll(r, 1, 0), pltpu.roll(r, te - 1, 0)
                r = conv(up * keep_up, dn * keep_dn)
            else:
                r = _dot(hb, w_ref[:, c0:c0 + cw])
            outs[oi][:, c0 - s0:c0 - s0 + cw] = r.astype(outs[oi].dtype)


def _project(lay, srcs, norm_w, mods, layer, k_shift, k_scale, w, splits, out_dtypes, *,
             conv=None, skip=0, name):
    d = srcs[0].shape[1]
    tm = lay.tm
    n_tiles = lay.tiles - skip
    row_fn = lambda j: lay.mod_row(j + skip)
    in_specs = _tok_specs(lay, srcs, tm)
    args = list(srcs)
    conv_idx = None
    if conv is not None:
        assert skip == 0
        conv_idx, conv_w, conv_b = conv
        in_specs += _tok_specs(lay, srcs, SUBLANES, shift=-1)
        in_specs += _tok_specs(lay, srcs, SUBLANES, shift=tm // SUBLANES)
        args += list(srcs) * 2
    in_specs += [_full(norm_w), _mod_spec(d, layer, k_shift, row_fn),
                 _mod_spec(d, layer, k_scale, row_fn), _layer_spec(w, layer if w.shape[0] > 1 else 0)]
    args += [norm_w, mods, mods, w]
    if conv is not None:
        in_specs += [_full(conv_w), _full(conv_b)]
        args += [conv_w, conv_b]
    out_specs = [_rows_spec(tm, sw, lambda j: j) for (_, sw) in splits]
    out_shape = [jax.ShapeDtypeStruct((n_tiles * tm, sw), dt) for (_, sw), dt in zip(splits, out_dtypes)]
    kern = functools.partial(_proj_kernel, lay=lay, n_src=len(srcs), splits=tuple(splits),
                             conv_idx=conv_idx, skip=skip)
    return pl.pallas_call(
        kern, grid=(n_tiles,), in_specs=in_specs, out_specs=out_specs, out_shape=out_shape,
        compiler_params=_params(), name=name,
    )(*args)


SSD_TABLES = ("w", "w_t", "dts", "dte", "ew", "etot")


def _ssd_prep(dt_ref, dtb_ref, alog_ref, *, reverse):
    q = CHUNK
    r_i = lax.broadcasted_iota(jnp.int32, (q, q), 0)
    c_i = lax.broadcasted_iota(jnp.int32, (q, q), 1)
    tri = (c_i >= r_i) if reverse else (c_i <= r_i)
    dts_t = _softplus(dt_ref[...] + dtb_ref[...])
    la_t = dts_t * (-jnp.exp(alog_ref[...]))
    yield
    w_t = jnp.dot(tri.astype(F32), la_t, precision=HIGHEST, preferred_element_type=F32)
    yield
    w = w_t.T
    dts = dts_t.T
    yield
    tot = w[:, 0:1] if reverse else w[:, q - 1:q]
    dte = jnp.exp(tot - w)
    ew = jnp.exp(w)
    etot = jnp.broadcast_to(jnp.exp(tot), (LANES, SSM_STATE))
    return dict(w=w, w_t=w_t, dts=dts, dte=dte, ew=ew, etot=etot)


def _ssd_mask(reverse):
    r_i = lax.broadcasted_iota(jnp.int32, (CHUNK, CHUNK), 0)
    c_i = lax.broadcasted_iota(jnp.int32, (CHUNK, CHUNK), 1)
    return (r_i >= c_i) if reverse else (r_i <= c_i)


def _ssd_group(p, xbc_ref, st_ref, g, *, reverse, heads):
    hd, ng, ns = SSM_HEAD_DIM, SSM_GROUPS, SSM_STATE
    hpg = heads // ng
    inner = heads * hd
    dir_off = heads if reverse else 0
    w, w_t = p["w"], p["w_t"]
    bb = xbc_ref[:, inner + g * ns: inner + (g + 1) * ns]
    cb = xbc_ref[:, inner + ng * ns + g * ns: inner + ng * ns + (g + 1) * ns]
    cbt = _dot_nt(bb, cb)
    st_g = st_ref[g]
    yint = _dot_nt(st_g.astype(BF16), cb)
    y_parts, xdec_parts = [], []
    for pr in range(hpg // 2):
        col = (g * hpg + 2 * pr) * hd
        blk_t = xbc_ref[:, col:col + 2 * hd].astype(F32).T
        for half in range(2):
            r = 2 * pr + half
            hh = dir_off + g * hpg + r
            xd = blk_t[half * hd:(half + 1) * hd, :] * p["dts"][hh:hh + 1, :]
            expo = jnp.where(p["mask_sl"], w[hh:hh + 1, :] - w_t[:, hh:hh + 1], -jnp.inf)
            mt = (cbt * jnp.exp(expo)).astype(BF16)
            y_h = (_dot(xd.astype(BF16), mt)
                   + yint[r * hd:(r + 1) * hd, :] * p["ew"][hh:hh + 1, :])
            y_parts.append(y_h)
            xdec_parts.append((xd * p["dte"][hh:hh + 1, :]).astype(BF16))
    xdec = jnp.concatenate(xdec_parts, axis=0)
    sloc = _dot(xdec, bb)
    new_state = [st_g[r * hd:(r + 1) * hd, :] * p["etot"][dir_off + g * hpg + r:dir_off + g * hpg + r + 1, :]
                 + sloc[r * hd:(r + 1) * hd, :] for r in range(hpg)]
    y_blocks = [jnp.concatenate(y_parts[2 * pr:2 * pr + 2], axis=0).T for pr in range(hpg // 2)]
    return y_blocks, new_state


def _ssd_kernel(xf_ref, dtf_ref, dtf_next_ref, xb_ref, dtb2_ref, dtb_next_ref, dtb_ref, alog_ref,
                yf_ref, yb_ref, stf_ref, stb_ref, tabf_ref, tabb_ref, *, heads):
    def park(tab_ref, tables):
        for k, name in enumerate(SSD_TABLES):
            tab_ref[k] = tables[name]

    @pl.when(pl.program_id(1) == 0)
    def _():
        stf_ref[...] = jnp.zeros_like(stf_ref)
        stb_ref[...] = jnp.zeros_like(stb_ref)
        first = _lockstep([_ssd_prep(dtf_ref, dtb_ref, alog_ref, reverse=False),
                           _ssd_prep(dtb2_ref, dtb_ref, alog_ref, reverse=True)])
        park(tabf_ref, first[0])
        park(tabb_ref, first[1])

    hd = SSM_HEAD_DIM
    hpg = heads // SSM_GROUPS
    chains = []
    for (tab_ref, x_ref, st_ref, y_ref, rev) in ((tabf_ref, xf_ref, stf_ref, yf_ref, False),
                                                 (tabb_ref, xb_ref, stb_ref, yb_ref, True)):
        p = {name: tab_ref.at[k] for k, name in enumerate(SSD_TABLES)}
        p["mask_sl"] = _ssd_mask(rev)
        chains.append((p, x_ref, st_ref, y_ref, rev))
    upcoming = _lockstep_inner([_ssd_prep(dtf_next_ref, dtb_ref, alog_ref, reverse=False),
                                _ssd_prep(dtb_next_ref, dtb_ref, alog_ref, reverse=True)])
    nxt = None
    for g in range(SSM_GROUPS):
        results = [_ssd_group(p, x_ref, st_ref, g, reverse=rev, heads=heads)
                   for (p, x_ref, st_ref, _, rev) in chains]
        if nxt is None:
            try:
                next(upcoming)
            except StopIteration as done:
                nxt = done.value
        for (_, _, st_ref, y_ref, _), (y_blocks, new_state) in zip(chains, results):
            for r, s_new in enumerate(new_state):
                st_ref[g, r * hd:(r + 1) * hd, :] = s_new
            for pr, blk in enumerate(y_blocks):
                col = (g * hpg + 2 * pr) * hd
                y_ref[:, col:col + 2 * hd] = blk.astype(y_ref.dtype)
    while nxt is None:
        try:
            next(upcoming)
        except StopIteration as done:
            nxt = done.value
    park(tabf_ref, nxt[0])
    park(tabb_ref, nxt[1])


def _chunk_block(lay, reverse):
    ncc, nlc = lay.ctx_chunks, lay.lat_chunks

    def block(b, i):
        in_ctx = i < ncc
        c_ctx = (ncc - 1 - i) if reverse else i
        c_lat = (nlc - 1 - (i - ncc)) if reverse else (i - ncc)
        return jnp.where(in_ctx, b * ncc + c_ctx, lay.bsz * ncc + b * nlc + c_lat)

    return block


def _ssd_scan(lay, xbc, dt, dtb, alog, *, heads):
    cch = xbc.shape[1]
    inner = heads * SSM_HEAD_DIM
    assert CHUNK == LANES == SSM_STATE
    fwd, bwd = _chunk_block(lay, False), _chunk_block(lay, True)
    n_steps = lay.ctx_chunks + lay.lat_chunks
    after = lambda order: (lambda b, i: order(b, jnp.minimum(i + 1, n_steps - 1)))
    blk = lambda width, order: pl.BlockSpec((CHUNK, width), lambda b, i: (order(b, i), 0))
    state = pltpu.VMEM((SSM_GROUPS, inner // SSM_GROUPS, SSM_STATE), F32)
    tables = pltpu.VMEM((len(SSD_TABLES), LANES, CHUNK), F32)
    return pl.pallas_call(
        functools.partial(_ssd_kernel, heads=heads),
        grid=(lay.bsz, n_steps),
        in_specs=[blk(cch, fwd), blk(LANES, fwd), blk(LANES, after(fwd)),
                  blk(cch, bwd), blk(LANES, bwd), blk(LANES, after(bwd)),
                  pl.BlockSpec((1, LANES), lambda b, i: (0, 0)),
                  pl.BlockSpec((1, LANES), lambda b, i: (0, 0))],
        out_specs=[blk(inner, fwd), blk(inner, bwd)],
        out_shape=[jax.ShapeDtypeStruct((lay.rows, inner), BF16)] * 2,
        scratch_shapes=[state, state, tables, tables],
        compiler_params=_params(2),
        name="ssd_scan",
    )(xbc, dt, dt, xbc, dt, dt, dtb, alog)


def _ssm_out_kernel(*refs, lay):
    yf_ref, yb_ref, xs_ref, z_ref, xc_ref, xl_ref, g_ref, d_ref, nw_ref, w_ref, o_ref = refs
    is_ctx = pl.program_id(0) < lay.ctx_tiles
    inner = z_ref.shape[1]
    acc = None
    sumsq = None
    for (c0, cw) in _col_chunks(0, inner, OUT_COL_CHUNK):
        cs = slice(c0, c0 + cw)
        z = z_ref[:, cs].astype(F32)
        y = (yf_ref[:, cs].astype(F32) + yb_ref[:, cs].astype(F32)
             + d_ref[:, cs] * xs_ref[:, cs].astype(F32)) * _silu(z)
        sq = y * y
        part_sq = functools.reduce(lambda u, v: u + v, [sq[:, k:k + LANES] for k in range(0, cw, LANES)])
        sumsq = part_sq if sumsq is None else sumsq + part_sq
        part = _dot((y * nw_ref[:, cs]).astype(BF16), w_ref[cs, :])
        acc = part if acc is None else acc + part
    scale = lax.rsqrt(jnp.sum(sumsq, axis=-1, keepdims=True) * (1.0 / inner) + NORM_EPS)
    resid = jnp.where(is_ctx, xc_ref[...], xl_ref[...])
    o_ref[...] = resid + g_ref[...] * (scale * acc)


def _ssm_out(lay, yf, yb, xbc, z, srcs, mods, layer, d_row, norm_w, w_out):
    inner = yf.shape[1]
    d = srcs[0].shape[1]
    tm = lay.tm
    tok = lambda c: _rows_spec(tm, c, lambda j: j)
    return pl.pallas_call(
        functools.partial(_ssm_out_kernel, lay=lay), grid=(lay.tiles,),
        in_specs=[tok(inner), tok(inner), tok(inner), tok(inner)]
        + _tok_specs(lay, srcs, tm)
        + [_mod_spec(d, layer, 2, lay.mod_row), _full(d_row), _full(norm_w), _layer_spec(w_out, 0)],
        out_specs=tok(d),
        out_shape=jax.ShapeDtypeStruct((lay.rows, d), F32),
        compiler_params=_params(), name="ssm_out",
    )(yf, yb, xbc, z, *srcs, mods, d_row, norm_w, w_out)


MLSTM_TABLES = ("wm_t", "rowterm", "inter_t", "floor_t", "e_r")
MLSTM_ROWS = ("a_prev", "a_loc", "m_new")


def _mlstm_keep(reverse):
    r_i = lax.broadcasted_iota(jnp.int32, (CHUNK, CHUNK), 0)
    c_i = lax.broadcasted_iota(jnp.int32, (CHUNK, CHUNK), 1)
    return (c_i >= r_i) if reverse else (c_i <= r_i)


def _mlstm_prep(gt_ref, gb_ref, m_prev, *, reverse):
    q = CHUNK
    nh = MLSTM_HEADS
    d = 1 if reverse else 0

    r_i = lax.broadcasted_iota(jnp.int32, (q, q), 0)
    c_i = lax.broadcasted_iota(jnp.int32, (q, q), 1)
    keep = _mlstm_keep(reverse)

    gt = GATE_CAP * jnp.tanh((gt_ref[...] + gb_ref[...]) / GATE_CAP)
    lane_ok = c_i < nh
    ig_sh = (LANES - nh * d) % LANES
    fg_sh = LANES - (2 * nh + nh * d)
    ig_t = gt if ig_sh == 0 else pltpu.roll(gt, ig_sh, 1)
    ig_t = jnp.where(lane_ok, ig_t, 0.0)
    fr_t = pltpu.roll(gt, fg_sh, 1)
    lf_t = jnp.where(lane_ok, -_softplus(-fr_t), 0.0)
    yield
    wv_t = jnp.dot(keep.astype(F32), lf_t, precision=HIGHEST, preferred_element_type=F32)
    ftot = jnp.sum(lf_t, axis=0, keepdims=True)
    yield
    wend_t = ftot - wv_t + ig_t
    mloc = jnp.max(wend_t, axis=0, keepdims=True)
    e_t = jnp.exp(wend_t - mloc)
    m_new = jnp.maximum(ftot + m_prev, mloc)
    a_prev = jnp.exp(ftot + m_prev - m_new)
    a_loc = jnp.exp(mloc - m_new)
    run = ig_t - wv_t
    sh = 1
    while sh < q:
        if reverse:
            moved, ok = pltpu.roll(run, q - sh, 0), r_i < q - sh
        else:
            moved, ok = pltpu.roll(run, sh, 0), r_i >= sh
        run = jnp.maximum(run, jnp.where(ok, moved, -jnp.inf))
        sh *= 2
    yield
    gcol_t = wv_t + m_prev
    mcomb_t = jnp.maximum(gcol_t, wv_t + run)
    wv = wv_t.T
    return dict(wm_t=wv_t - mcomb_t, rowterm=ig_t.T - wv, inter_t=jnp.exp(gcol_t - mcomb_t),
                floor_t=jnp.exp(-mcomb_t), e_r=e_t.T, a_prev=a_prev, a_loc=a_loc, m_new=m_new)


def _mlstm_head(p, qk_ref, v_ref, c_ref, h):
    q = CHUNK
    nh = MLSTM_HEADS
    dk = qk_ref.shape[-1] // (2 * nh)
    dv = v_ref.shape[-1] // nh
    qb = qk_ref[:, h * dk:(h + 1) * dk]
    kt = (qk_ref[:, nh * dk + h * dk: nh * dk + (h + 1) * dk].astype(F32) * (dk ** -0.5)).T
    vaug = jnp.concatenate([v_ref[:, h * dv:(h + 1) * dv], jnp.ones((q, LANES), BF16)], axis=1)
    caug = c_ref[h]
    lanes = lambda col: jnp.broadcast_to(col, (q, LANES))
    pm = jnp.exp(jnp.where(p["keep"], p["wm_t"][:, h:h + 1] + p["rowterm"][h:h + 1, :], -jnp.inf))
    inter = lanes(p["inter_t"][:, h:h + 1])
    floor = lanes(p["floor_t"][:, h:h + 1])
    yield
    s = (_dot(qb, kt.astype(BF16)) * pm).astype(BF16)
    qc = _dot(qb, caug.astype(BF16))
    yield
    sv = _dot(s, vaug)
    ap = p["a_prev"][:, h:h + 1]
    al = p["a_loc"][:, h:h + 1]
    c_new = ap * caug + al * _dot((kt * p["e_r"][h:h + 1, :]).astype(BF16), vaug)
    yield
    blocks = [sv[:, c:c + LANES] + inter * qc[:, c:c + LANES] for c in range(0, dv + LANES, LANES)]
    rden = 1.0 / jnp.maximum(jnp.abs(blocks[-1]), floor)
    h_out = jnp.concatenate([b * rden for b in blocks[:-1]], axis=1)
    return h_out, c_new


def _mlstm_kernel(qkf_ref, vf_ref, gtf_ref, gtf_next_ref, qkb_ref, vb_ref, gtb_ref, gtb_next_ref,
                  gb_ref, hf_ref, hb_ref, cf_ref, tabf_ref, rowf_ref, cb_ref, tabb_ref, rowb_ref):
    def park(tab_ref, row_ref, tables):
        for k, name in enumerate(MLSTM_TABLES):
            tab_ref[k] = tables[name]
        for k, name in enumerate(MLSTM_ROWS):
            row_ref[k] = tables[name]

    @pl.when(pl.program_id(1) == 0)
    def _():
        cf_ref[...] = jnp.zeros_like(cf_ref)
        cb_ref[...] = jnp.zeros_like(cb_ref)
        zero = jnp.zeros((1, LANES), F32)
        first = _lockstep([_mlstm_prep(gtf_ref, gb_ref, zero, reverse=False),
                           _mlstm_prep(gtb_ref, gb_ref, zero, reverse=True)])
        park(tabf_ref, rowf_ref, first[0])
        park(tabb_ref, rowb_ref, first[1])

    dv = vf_ref.shape[-1] // MLSTM_HEADS
    chains = []
    for (tab_ref, row_ref, qk_ref, v_ref, c_ref, h_ref, rev) in (
            (tabf_ref, rowf_ref, qkf_ref, vf_ref, cf_ref, hf_ref, False),
            (tabb_ref, rowb_ref, qkb_ref, vb_ref, cb_ref, hb_ref, True)):
        p = {name: tab_ref[k] for k, name in enumerate(MLSTM_TABLES)}
        p.update({name: row_ref[k] for k, name in enumerate(MLSTM_ROWS)})
        p["keep"] = _mlstm_keep(rev)
        chains.append((p, qk_ref, v_ref, c_ref, h_ref))
    units = [(p, qk_ref, v_ref, c_ref, h_ref, h)
             for (p, qk_ref, v_ref, c_ref, h_ref) in chains for h in range(MLSTM_HEADS)]
    results = _lockstep(
        [_mlstm_head(p, qk_ref, v_ref, c_ref, h) for (p, qk_ref, v_ref, c_ref, _, h) in units]
        + [_mlstm_prep(gtf_next_ref, gb_ref, chains[0][0]["m_new"], reverse=False),
           _mlstm_prep(gtb_next_ref, gb_ref, chains[1][0]["m_new"], reverse=True)])
    for (_, _, _, c_ref, h_ref, h), (h_out, c_new) in zip(units, results):
        h_ref[:, h * dv:(h + 1) * dv] = h_out.astype(h_ref.dtype)
        c_ref[h] = c_new
    park(tabf_ref, rowf_ref, results[-2])
    park(tabb_ref, rowb_ref, results[-1])


def _mlstm_scan(lay, qk, v, gates, gate_b):
    qkw, vw = qk.shape[1], v.shape[1]
    nh = MLSTM_HEADS
    dk, dv = qkw // (2 * nh), vw // nh
    ncc, nlc = lay.ctx_chunks, lay.lat_chunks
    fwd, bwd = _chunk_block(lay, False), _chunk_block(lay, True)
    out_f = lambda b, i: b * nlc + jnp.maximum(i - ncc, 0)
    out_b = lambda b, i: b * nlc + nlc - 1 - jnp.maximum(i - ncc, 0)
    assert CHUNK == LANES
    n_steps = ncc + nlc
    after = lambda order: (lambda b, i: order(b, jnp.minimum(i + 1, n_steps - 1)))
    blk = lambda width, order: pl.BlockSpec((CHUNK, width), lambda b, i: (order(b, i), 0))
    ins = lambda order: [blk(qkw, order), blk(vw, order), blk(LANES, order), blk(LANES, after(order))]
    state = [pltpu.VMEM((nh, dk, dv + LANES), F32),
             pltpu.VMEM((len(MLSTM_TABLES), CHUNK, LANES), F32),
             pltpu.VMEM((len(MLSTM_ROWS), 1, LANES), F32)]
    return pl.pallas_call(
        _mlstm_kernel,
        grid=(lay.bsz, n_steps),
        in_specs=ins(fwd) + ins(bwd) + [pl.BlockSpec((1, LANES), lambda b, i: (0, 0))],
        out_specs=[blk(vw, out_f), blk(vw, out_b)],
        out_shape=[jax.ShapeDtypeStruct((lay.lat_rows, vw), BF16)] * 2,
        scratch_shapes=state + state,
        compiler_params=_params(2),
        name="mlstm_scan",
    )(qk, v, gates, gates, qk, v, gates, gates, gate_b)


def _mlstm_out_up_kernel(hf_ref, hb_ref, o_ref_in, x_ref, g_ref, nw_ref, w_ref,
                         n2_ref, sh_ref, sc_ref, wup_ref, x_out_ref, a_ref, gl_ref):
    nh = MLSTM_HEADS
    dv = hf_ref.shape[-1] // nh
    for r0 in range(0, hf_ref.shape[0], OUT_ROW_BLOCK):
        rows = slice(r0, r0 + OUT_ROW_BLOCK)
        parts = []
        for h in range(nh):
            hh = (hf_ref[rows, h * dv:(h + 1) * dv].astype(F32)
                  + hb_ref[rows, h * dv:(h + 1) * dv].astype(F32))
            parts.append(_rms_scale(hh))
        hn = jnp.concatenate(parts, axis=1)
        y = (hn * nw_ref[...] * jax.nn.sigmoid(o_ref_in[rows, :].astype(F32))).astype(BF16)
        x_out_ref[rows, :] = x_ref[rows, :] + g_ref[...] * _dot(y, w_ref[...])
    h2 = _rms_scale(x_out_ref[...]) * n2_ref[...]
    hb2 = (h2 * (1.0 + sc_ref[...]) + sh_ref[...]).astype(BF16)
    f = a_ref.shape[1]
    for out_ref, s0 in ((a_ref, 0), (gl_ref, f)):
        for (c0, cw) in _col_chunks(s0, f):
            out_ref[:, c0 - s0:c0 - s0 + cw] = _dot(hb2, wup_ref[:, c0:c0 + cw]).astype(out_ref.dtype)


def _mlstm_out_up(lay, hf, hb, o_gate, xall, mods, layer, norm_w, w_out, norm2_w, w_up):
    vw = hf.shape[1]
    d = xall.shape[1]
    f = w_up.shape[2] // 2
    tm = lay.tm
    lat = lambda c: _rows_spec(tm, c, lambda j: j)
    allt = lambda c: _rows_spec(tm, c, lambda j: j + lay.ctx_tiles)
    row_fn = lambda j: j // lay.lat_tiles_per_batch
    return pl.pallas_call(
        _mlstm_out_up_kernel, grid=(lay.lat_tiles,),
        in_specs=[lat(vw), lat(vw), allt(vw), allt(d),
                  _mod_spec(d, layer, 2, row_fn), _full(norm_w), _layer_spec(w_out, 0),
                  _full(norm2_w), _mod_spec(d, layer, 3, row_fn), _mod_spec(d, layer, 4, row_fn),
                  _layer_spec(w_up, layer)],
        out_specs=[lat(d), lat(f), lat(f)],
        out_shape=[jax.ShapeDtypeStruct((lay.lat_rows, d), F32),
                   jax.ShapeDtypeStruct((lay.lat_rows, f), F32),
                   jax.ShapeDtypeStruct((lay.lat_rows, f), BF16)],
        compiler_params=_params(), name="mlstm_out_ffn_up",
    )(hf, hb, o_gate, xall, mods, norm_w, w_out, norm2_w, mods, mods, w_up)


def _ffn_down_kernel(*refs, lay, skip, final_norm):
    if final_norm:
        a_ref, ap_ref, an_ref, gl_ref, x_ref, g_ref, cw_ref, cb_ref, w_ref, fw_ref, o_ref = refs
    else:
        a_ref, ap_ref, an_ref, gl_ref, x_ref, g_ref, cw_ref, cb_ref, w_ref, o_ref = refs
    tm, f = a_ref.shape
    gw = GRID_WIDTH
    j = pl.program_id(0) + skip
    is_ctx = j < lay.ctx_tiles
    pos = lay.batch_pos(j)
    has_prev = jnp.logical_and(jnp.logical_not(is_ctx), pos > 0)
    has_next = jnp.logical_and(jnp.logical_not(is_ctx), pos < lay.lat_tiles_per_batch - 1)

    cw = FFN_COL_CHUNK
    vert = jnp.where(is_ctx, 0.0, 1.0)
    sb = FFN_SHIFT_BLOCK
    ri = lax.broadcasted_iota(jnp.int32, (sb, sb), 0)
    ci = lax.broadcasted_iota(jnp.int32, (sb, sb), 1)
    colpos = jnp.where(is_ctx, ri % lay.n_ctx, ri & (gw - 1))
    last_col = jnp.where(is_ctx, lay.n_ctx - 1, gw - 1)
    take_left = jnp.where(colpos != 0, jnp.where(ci == ri - 1, 1.0, 0.0), 0.0)
    take_right = jnp.where(colpos != last_col, jnp.where(ci == ri + 1, 1.0, 0.0), 0.0)
    shift = jnp.concatenate([take_left, take_right], axis=1).astype(BF16)

    def taps(c0):
        cs = slice(c0, c0 + cw)
        up = jnp.where(has_prev, ap_ref[:, cs], 0.0)
        dn = jnp.where(has_next, an_ref[:, cs], 0.0)
        slabs = [jnp.concatenate([up, a_ref[0:tm - gw, cs]], axis=0), a_ref[:, cs],
                 jnp.concatenate([a_ref[gw:tm, cs], dn], axis=0)]

        def vsum(dj, parts, dtype):
            k = lambda di: 3 * (di + 1) + (dj + 1)
            return (parts[0] * (cw_ref[k(-1):k(-1) + 1, cs] * vert).astype(dtype)
                    + parts[1] * cw_ref[k(0):k(0) + 1, cs].astype(dtype)
                    + parts[2] * (cw_ref[k(1):k(1) + 1, cs] * vert).astype(dtype))

        slabs_b = [s.astype(BF16) for s in slabs]
        left, right = vsum(-1, slabs_b, BF16), vsum(1, slabs_b, BF16)
        sides = jnp.concatenate(
            [_dot(shift, jnp.concatenate([left[r0:r0 + sb], right[r0:r0 + sb]], axis=0))
             for r0 in range(0, tm, sb)], axis=0)
        return vsum(0, slabs, F32) + cb_ref[:, cs], sides

    def finish(c0, centre, sides, acc_out):
        cs = slice(c0, c0 + cw)
        mid = (_silu(centre + sides) * gl_ref[:, cs].astype(F32)).astype(BF16)
        part = _dot(mid, w_ref[cs, :])
        return part if acc_out is None else acc_out + part

    acc_out = None
    pending = None
    for c0 in range(0, f, cw):
        cur = (c0,) + taps(c0)
        if pending is not None:
            acc_out = finish(*pending, acc_out)
        pending = cur
    acc_out = finish(*pending, acc_out)
    out = x_ref[...] + g_ref[...] * acc_out
    if final_norm:
        out = _rms_scale(out) * fw_ref[...]
    o_ref[...] = out


def _ffn_down(lay, a, gl, xres, mods, layer, conv_w9, conv_b, w_down, *, skip=0, final_w=None):
    tt, f = a.shape
    d = xres.shape[1]
    tm = lay.tm
    gw = GRID_WIDTH
    assert f % FFN_COL_CHUNK == 0 and tm % FFN_SHIFT_BLOCK == 0
    assert FFN_SHIFT_BLOCK % gw == 0 and FFN_SHIFT_BLOCK % lay.n_ctx == 0
    n_tiles = tt // tm
    hb = tm // gw
    tok = lambda c: _rows_spec(tm, c, lambda j: j)
    in_specs = [
        tok(f),
        _rows_spec(gw, f, lambda j: jnp.maximum(j * hb - 1, 0)),
        _rows_spec(gw, f, lambda j: jnp.minimum((j + 1) * hb, tt // gw - 1)),
        tok(f), tok(d), _mod_spec(d, layer, 5, lambda j: lay.mod_row(j + skip)),
        _full(conv_w9), _full(conv_b), _layer_spec(w_down, layer)]
    args = [a, a, a, gl, xres, mods, conv_w9, conv_b, w_down]
    if final_w is not None:
        in_specs.append(_full(final_w))
        args.append(final_w)
    kern = functools.partial(_ffn_down_kernel, lay=lay, skip=skip, final_norm=final_w is not None)
    return pl.pallas_call(
        kern, grid=(n_tiles,), in_specs=in_specs, out_specs=tok(d),
        out_shape=jax.ShapeDtypeStruct((tt, d), F32),
        compiler_params=_params(), name=f"ffn_down_l{layer}",
    )(*args)


def _pad_cols(a, n):
    return jnp.pad(a, [(0, 0)] * (a.ndim - 1) + [(0, n - a.shape[-1])])


def kernel(x, c, ctx, c_ctx, ada_w, ada_b, norm1_w, norm2_w, ssm_w_in, ssm_conv_w, ssm_conv_b, ssm_dt_bias, ssm_a_log, ssm_d, ssm_norm_w, ssm_w_out, mlstm_w_in, mlstm_conv_w, mlstm_conv_b, mlstm_gate_b, mlstm_norm_w, mlstm_w_out, ffn_w_up, ffn_conv_w, ffn_conv_b, ffn_w_down, final_norm_w):
    bsz, seq, d = x.shape
    n_ctx = ctx.shape[1]
    depth = ada_w.shape[0]
    assert depth == 2 and ssm_w_in.shape[0] == 1 and mlstm_w_in.shape[0] == 1
    lay = _Layout(bsz, n_ctx, seq)
    x_flat = x.reshape(bsz * seq, d)
    ctx_flat = ctx.reshape(bsz * n_ctx, d)

    n_rows = -(-(bsz + 1) // SUBLANES) * SUBLANES
    cvec = jnp.zeros((n_rows, d), F32).at[:bsz].set(c).at[bsz].set(c_ctx)
    mods = _ada(cvec, ada_w, ada_b).reshape(depth, n_rows, 6, 1, d)

    w_up = ffn_w_up.astype(BF16)
    w_down = ffn_w_down.astype(BF16)
    f = ffn_conv_b.shape[1]
    conv9 = ffn_conv_w.reshape(depth, 9, f)

    heads = ssm_d.shape[1]
    assert 2 * heads <= LANES
    inner = heads * SSM_HEAD_DIM
    bc_w = 2 * SSM_GROUPS * SSM_STATE
    conv_ch = inner + bc_w
    w_in = _pad_cols(ssm_w_in, 2 * inner + bc_w + LANES).astype(BF16)
    splits = [(0, inner), (inner, conv_ch), (inner + conv_ch, LANES)]
    z, xbc, dt = _project(
        lay, (ctx_flat, x_flat), norm1_w[0:1], mods, 0, 0, 1, w_in, splits, (BF16, BF16, F32),
        conv=(1, ssm_conv_w[0], ssm_conv_b[0:1]), name="in_proj_l0")
    dtb = _pad_cols(ssm_dt_bias[0].reshape(1, -1), LANES)
    alog = _pad_cols(ssm_a_log[0].reshape(1, -1), LANES)
    y_f, y_b = _ssd_scan(lay, xbc, dt, dtb, alog, heads=heads)
    d_row = jnp.repeat(ssm_d[0], SSM_HEAD_DIM).reshape(1, inner)
    xall = _ssm_out(lay, y_f, y_b, xbc, z, (ctx_flat, x_flat), mods, 0, d_row, ssm_norm_w[0:1],
                    ssm_w_out.astype(BF16))

    a, gl = _project(lay, (xall,), norm2_w[0:1], mods, 0, 3, 4, w_up, [(0, f), (f, f)], (F32, BF16),
                     name="ffn_up_l0")
    xall = _ffn_down(lay, a, gl, xall, mods, 0, conv9[0], ffn_conv_b[0:1], w_down)

    qkw = mlstm_conv_b.shape[1]
    vw = mlstm_norm_w.shape[1]
    w_in = _pad_cols(mlstm_w_in, qkw + 2 * vw + LANES).astype(BF16)
    splits = [(0, qkw), (qkw, vw), (qkw + vw, vw), (qkw + 2 * vw, LANES)]
    qk, v, o_gate, gates = _project(
        lay, (xall,), norm1_w[1:2], mods, 1, 0, 1, w_in, splits, (BF16, BF16, BF16, F32),
        conv=(0, mlstm_conv_w[0], mlstm_conv_b[0:1]), name="in_proj_l1")
    gate_b = _pad_cols(mlstm_gate_b[0:1], LANES)
    h_f, h_b = _mlstm_scan(lay, qk, v, gates, gate_b)
    xlat, a, gl = _mlstm_out_up(lay, h_f, h_b, o_gate, xall, mods, 1, mlstm_norm_w[0:1],
                                mlstm_w_out.astype(BF16), norm2_w[1:2], w_up)
    out = _ffn_down(lay, a, gl, xlat, mods, 1, conv9[1], ffn_conv_b[1:2], w_down,
                    skip=lay.ctx_tiles, final_w=final_norm_w.reshape(1, d))
    return out.reshape(bsz, seq, d)
```

```python
import functools

import jax
import jax.numpy as jnp
from jax import lax
from jax.experimental import pallas as pl
from jax.experimental.pallas import tpu as pltpu

F32 = jnp.float32
BF16 = jnp.bfloat16
HIGHEST = lax.Precision.HIGHEST

NORM_EPS = 1e-6
GRID_WIDTH = 64
SSM_HEAD_DIM = 64
SSM_GROUPS = 4
SSM_STATE = 128
MLSTM_HEADS = 4
GATE_CAP = 15.0
CHUNK = 128

LANES = 128
SUBLANES = 8
TOKEN_TILE = 512
COL_CHUNK = 512
FFN_COL_CHUNK = 256
FFN_SHIFT_BLOCK = 256
OUT_ROW_BLOCK = 256
OUT_COL_CHUNK = 256
VMEM_LIMIT = 56 * 1024 * 1024


class _Layout:
    def __init__(self, bsz, n_ctx, seq, tm=TOKEN_TILE):
        assert tm % n_ctx == 0 and (bsz * n_ctx) % tm == 0 and seq % tm == 0
        assert n_ctx % CHUNK == 0 and seq % CHUNK == 0 and seq % GRID_WIDTH == 0
        self.bsz, self.n_ctx, self.seq, self.tm = bsz, n_ctx, seq, tm
        self.ctx_rows, self.lat_rows = bsz * n_ctx, bsz * seq
        self.rows = self.ctx_rows + self.lat_rows
        self.ctx_tiles = self.ctx_rows // tm
        self.lat_tiles_per_batch = seq // tm
        self.lat_tiles = bsz * self.lat_tiles_per_batch
        self.tiles = self.ctx_tiles + self.lat_tiles
        self.ctx_chunks, self.lat_chunks = n_ctx // CHUNK, seq // CHUNK

    def mod_row(self, j):
        return jnp.where(j < self.ctx_tiles, self.bsz, (j - self.ctx_tiles) // self.lat_tiles_per_batch)

    def batch_pos(self, j):
        return (j - self.ctx_tiles) % self.lat_tiles_per_batch


def _params(n_axes=1):
    return pltpu.CompilerParams(
        dimension_semantics=("arbitrary",) * n_axes, vmem_limit_bytes=VMEM_LIMIT)


def _col_chunks(start, width, step=COL_CHUNK):
    out, c = [], start
    while c < start + width:
        w = min(step, start + width - c)
        out.append((c, w))
        c += w
    return out


def _silu(v):
    half = 0.5 * v
    return half + half * jnp.tanh(half)


def _softplus(v):
    return jnp.maximum(v, 0.0) + jnp.log1p(jnp.exp(-jnp.abs(v)))


def _rms_scale(v):
    return v * lax.rsqrt(jnp.mean(v * v, axis=-1, keepdims=True) + NORM_EPS)


def _dot(a, b):
    return jnp.dot(a, b, preferred_element_type=F32)


def _dot_nt(a, b):
    return lax.dot_general(a, b, (((1,), (1,)), ((), ())), preferred_element_type=F32)


def _lockstep_inner(gens):
    results = [None] * len(gens)
    live = list(range(len(gens)))
    while live:
        for i in list(live):
            try:
                next(gens[i])
            except StopIteration as done:
                results[i] = done.value
                live.remove(i)
        yield
    return results


def _lockstep(gens):
    it = _lockstep_inner(gens)
    while True:
        try:
            next(it)
        except StopIteration as done:
            return done.value


def _full(a):
    return pl.BlockSpec(a.shape, lambda *_: (0,) * a.ndim)


def _layer_spec(a, layer):
    return pl.BlockSpec((None,) + a.shape[1:], lambda *_: (layer,) + (0,) * (a.ndim - 1),
                        pipeline_mode=pl.Buffered(1))


def _mod_spec(d, layer, k, row_fn):
    return pl.BlockSpec((None, None, None, 1, d), lambda j: (layer, row_fn(j), k, 0, 0))


def _rows_spec(rows, width, idx_fn):
    return pl.BlockSpec((rows, width), lambda j: (idx_fn(j), 0))


def _ada_kernel(c_ref, w_ref, b_ref, o_ref):
    s = _silu(c_ref[...])
    o_ref[...] = jnp.dot(s, w_ref[...], precision=HIGHEST, preferred_element_type=F32) + b_ref[...]


def _ada(cvec, ada_w, ada_b):
    depth, d, n = ada_w.shape
    rows = cvec.shape[0]
    tn = 512
    return pl.pallas_call(
        _ada_kernel,
        grid=(depth, n // tn),
        in_specs=[
            pl.BlockSpec((rows, d), lambda l, j: (0, 0)),
            pl.BlockSpec((None, d, tn), lambda l, j: (l, 0, j)),
            pl.BlockSpec((None, 1, tn), lambda l, j: (l, 0, j)),
        ],
        out_specs=pl.BlockSpec((None, rows, tn), lambda l, j: (l, 0, j)),
        out_shape=jax.ShapeDtypeStruct((depth, rows, n), F32),
        compiler_params=_params(2),
        name="ada_mod",
    )(cvec, ada_w, ada_b.reshape(depth, 1, n))


def _tok_specs(lay, arrs, rows, shift=0):
    per = lay.tm // rows
    if len(arrs) == 1:
        (a,) = arrs
        hi = a.shape[0] // rows - 1
        return [_rows_spec(rows, a.shape[1], lambda j: jnp.clip(j * per + shift, 0, hi))]
    ctx, lat = arrs
    hi_c = ctx.shape[0] // rows - 1
    hi_l = lat.shape[0] // rows - 1
    return [_rows_spec(rows, ctx.shape[1], lambda j: jnp.clip(j * per + shift, 0, hi_c)),
            _rows_spec(rows, lat.shape[1],
                       lambda j: jnp.clip((j - lay.ctx_tiles) * per + shift, 0, hi_l))]


def _tok_read(refs, is_ctx):
    if len(refs) == 1:
        return refs[0][...]
    return jnp.where(is_ctx, refs[0][...], refs[1][...])


def _proj_kernel(*refs, lay, n_src, splits, conv_idx, skip):
    refs = list(refs)
    x_refs = [refs.pop(0) for _ in range(n_src)]
    if conv_idx is not None:
        xp_refs = [refs.pop(0) for _ in range(n_src)]
        xn_refs = [refs.pop(0) for _ in range(n_src)]
    nw_ref, sh_ref, sc_ref, w_ref = refs[:4]
    refs = refs[4:]
    if conv_idx is not None:
        cw_ref, cb_ref = refs[:2]
        refs = refs[2:]
    outs = refs
    j = pl.program_id(0) + skip
    is_ctx = j < lay.ctx_tiles
    tm = lay.tm

    def prep(v):
        h = _rms_scale(v) * nw_ref[...]
        return h * (1.0 + sc_ref[...]) + sh_ref[...]

    h_main = prep(_tok_read(x_refs, is_ctx))
    hb = h_main.astype(BF16)
    if conv_idx is not None:
        pos = lay.batch_pos(j)
        is_first = jnp.logical_or(is_ctx, pos == 0)
        is_last = jnp.logical_or(is_ctx, pos == lay.lat_tiles_per_batch - 1)
        hp = jnp.where(is_first, 0.0, prep(_tok_read(xp_refs, is_ctx)))
        hn = jnp.where(is_last, 0.0, prep(_tok_read(xn_refs, is_ctx)))
        hext = jnp.concatenate([hp, h_main, hn], axis=0).astype(BF16)
        te = tm + 2 * SUBLANES
        seam_rows = lax.broadcasted_iota(jnp.int32, (te, 1), 0) - SUBLANES
        keep_up = jnp.where(jnp.logical_and(is_ctx, seam_rows % lay.n_ctx == 0), 0.0, 1.0)
        keep_dn = jnp.where(jnp.logical_and(is_ctx, seam_rows % lay.n_ctx == lay.n_ctx - 1), 0.0, 1.0)
    for oi, (s0, sw) in enumerate(splits):
        for (c0, cw) in _col_chunks(s0, sw):
            if oi == conv_idx:
                r = _dot(hext, w_ref[:, c0:c0 + cw])
                k0 = c0 - s0

                def conv(up, dn):
                    y = (up * cw_ref[0:1, k0:k0 + cw] + r * cw_ref[1:2, k0:k0 + cw]
                         + dn * cw_ref[2:3, k0:k0 + cw] + cb_ref[:, k0:k0 + cw])
                    return _silu(y[SUBLANES:SUBLANES + tm, :])

                up, dn = pltpu.roll(r, 1, 0), pltpu.roll(r, te - 1, 0)
                r = conv(up * keep_up, dn * keep_dn)
            else:
                r = _dot(hb, w_ref[:, c0:c0 + cw])
            outs[oi][:, c0 - s0:c0 - s0 + cw] = r.astype(outs[oi].dtype)


def _project(lay, srcs, norm_w, mods, layer, k_shift, k_scale, w, splits, out_dtypes, *,
             conv=None, skip=0, name):
    d = srcs[0].shape[1]
    tm = lay.tm
    n_tiles = lay.tiles - skip
    row_fn = lambda j: lay.mod_row(j + skip)
    in_specs = _tok_specs(lay, srcs, tm)
    args = list(srcs)
    conv_idx = None
    if conv is not None:
        assert skip == 0
        conv_idx, conv_w, conv_b = conv
        in_specs += _tok_specs(lay, srcs, SUBLANES, shift=-1)
        in_specs += _tok_specs(lay, srcs, SUBLANES, shift=tm // SUBLANES)
        args += list(srcs) * 2
    in_specs += [_full(norm_w), _mod_spec(d, layer, k_shift, row_fn),
                 _mod_spec(d, layer, k_scale, row_fn), _layer_spec(w, layer if w.shape[0] > 1 else 0)]
    args += [norm_w, mods, mods, w]
    if conv is not None:
        in_specs += [_full(conv_w), _full(conv_b)]
        args += [conv_w, conv_b]
    out_specs = [_rows_spec(tm, sw, lambda j: j) for (_, sw) in splits]
    out_shape = [jax.ShapeDtypeStruct((n_tiles * tm, sw), dt) for (_, sw), dt in zip(splits, out_dtypes)]
    kern = functools.partial(_proj_kernel, lay=lay, n_src=len(srcs), splits=tuple(splits),
                             conv_idx=conv_idx, skip=skip)
    return pl.pallas_call(
        kern, grid=(n_tiles,), in_specs=in_specs, out_specs=out_specs, out_shape=out_shape,
        compiler_params=_params(), name=name,
    )(*args)


SSD_TABLES = ("w", "w_t", "dts", "dte", "ew", "etot")


def _ssd_prep(dt_ref, dtb_ref, alog_ref, *, reverse):
    q = CHUNK
    r_i = lax.broadcasted_iota(jnp.int32, (q, q), 0)
    c_i = lax.broadcasted_iota(jnp.int32, (q, q), 1)
    tri = (c_i >= r_i) if reverse else (c_i <= r_i)
    dts_t = _softplus(dt_ref[...] + dtb_ref[...])
    la_t = dts_t * (-jnp.exp(alog_ref[...]))
    yield
    w_t = jnp.dot(tri.astype(F32), la_t, precision=HIGHEST, preferred_element_type=F32)
    yield
    w = w_t.T
    dts = dts_t.T
    yield
    tot = w[:, 0:1] if reverse else w[:, q - 1:q]
    dte = jnp.exp(tot - w)
    ew = jnp.exp(w)
    etot = jnp.broadcast_to(jnp.exp(tot), (LANES, SSM_STATE))
    return dict(w=w, w_t=w_t, dts=dts, dte=dte, ew=ew, etot=etot)


def _ssd_mask(reverse):
    r_i = lax.broadcasted_iota(jnp.int32, (CHUNK, CHUNK), 0)
    c_i = lax.broadcasted_iota(jnp.int32, (CHUNK, CHUNK), 1)
    return (r_i >= c_i) if reverse else (r_i <= c_i)


def _ssd_group(p, xbc_ref, st_ref, g, *, reverse, heads, skip_ref=None):
    hd, ng, ns = SSM_HEAD_DIM, SSM_GROUPS, SSM_STATE
    hpg = heads // ng
    inner = heads * hd
    dir_off = heads if reverse else 0
    w, w_t = p["w"], p["w_t"]
    bb = xbc_ref[:, inner + g * ns: inner + (g + 1) * ns]
    cb = xbc_ref[:, inner + ng * ns + g * ns: inner + ng * ns + (g + 1) * ns]
    cbt = _dot_nt(bb, cb)
    st_g = st_ref[g]
    yint = _dot_nt(st_g.astype(BF16), cb)
    y_parts, xdec_parts = [], []
    for pr in range(hpg // 2):
        col = (g * hpg + 2 * pr) * hd
        blk_t = xbc_ref[:, col:col + 2 * hd].astype(F32).T
        for half in range(2):
            r = 2 * pr + half
            hh = dir_off + g * hpg + r
            xd = blk_t[half * hd:(half + 1) * hd, :] * p["dts"][hh:hh + 1, :]
            expo = jnp.where(p["mask_sl"], w[hh:hh + 1, :] - w_t[:, hh:hh + 1], -jnp.inf)
            mt = (cbt * jnp.exp(expo)).astype(BF16)
            y_h = (_dot(xd.astype(BF16), mt)
                   + yint[r * hd:(r + 1) * hd, :] * p["ew"][hh:hh + 1, :])
            if skip_ref is not None:
                y_h = y_h + skip_ref[g * hpg + r] * blk_t[half * hd:(half + 1) * hd, :]
            y_parts.append(y_h)
            xdec_parts.append((xd * p["dte"][hh:hh + 1, :]).astype(BF16))
    xdec = jnp.concatenate(xdec_parts, axis=0)
    sloc = _dot(xdec, bb)
    new_state = [st_g[r * hd:(r + 1) * hd, :] * p["etot"][dir_off + g * hpg + r:dir_off + g * hpg + r + 1, :]
                 + sloc[r * hd:(r + 1) * hd, :] for r in range(hpg)]
    y_blocks = [jnp.concatenate(y_parts[2 * pr:2 * pr + 2], axis=0).T for pr in range(hpg // 2)]
    return y_blocks, new_state


def _ssd_kernel(xf_ref, dtf_ref, dtf_next_ref, xb_ref, dtb2_ref, dtb_next_ref, dtb_ref, alog_ref,
                d_ref, yf_ref, yb_ref, stf_ref, stb_ref, tabf_ref, tabb_ref, *, heads):
    def park(tab_ref, tables):
        for k, name in enumerate(SSD_TABLES):
            tab_ref[k] = tables[name]

    @pl.when(pl.program_id(1) == 0)
    def _():
        stf_ref[...] = jnp.zeros_like(stf_ref)
        stb_ref[...] = jnp.zeros_like(stb_ref)
        first = _lockstep([_ssd_prep(dtf_ref, dtb_ref, alog_ref, reverse=False),
                           _ssd_prep(dtb2_ref, dtb_ref, alog_ref, reverse=True)])
        park(tabf_ref, first[0])
        park(tabb_ref, first[1])

    hd = SSM_HEAD_DIM
    hpg = heads // SSM_GROUPS
    chains = []
    for (tab_ref, x_ref, st_ref, y_ref, rev) in ((tabf_ref, xf_ref, stf_ref, yf_ref, False),
                                                 (tabb_ref, xb_ref, stb_ref, yb_ref, True)):
        p = {name: tab_ref.at[k] for k, name in enumerate(SSD_TABLES)}
        p["mask_sl"] = _ssd_mask(rev)
        chains.append((p, x_ref, st_ref, y_ref, rev))
    upcoming = _lockstep_inner([_ssd_prep(dtf_next_ref, dtb_ref, alog_ref, reverse=False),
                                _ssd_prep(dtb_next_ref, dtb_ref, alog_ref, reverse=True)])
    nxt = None
    for g in range(SSM_GROUPS):
        results = [_ssd_group(p, x_ref, st_ref, g, reverse=rev, heads=heads,
                              skip_ref=None if rev else d_ref)
                   for (p, x_ref, st_ref, _, rev) in chains]
        if nxt is None:
            try:
                next(upcoming)
            except StopIteration as done:
                nxt = done.value
        for (_, _, st_ref, y_ref, _), (y_blocks, new_state) in zip(chains, results):
            for r, s_new in enumerate(new_state):
                st_ref[g, r * hd:(r + 1) * hd, :] = s_new
            for pr, blk in enumerate(y_blocks):
                col = (g * hpg + 2 * pr) * hd
                y_ref[:, col:col + 2 * hd] = blk.astype(y_ref.dtype)
    while nxt is None:
        try:
            next(upcoming)
        except StopIteration as done:
            nxt = done.value
    park(tabf_ref, nxt[0])
    park(tabb_ref, nxt[1])


def _chunk_block(lay, reverse):
    ncc, nlc = lay.ctx_chunks, lay.lat_chunks

    def block(b, i):
        in_ctx = i < ncc
        c_ctx = (ncc - 1 - i) if reverse else i
        c_lat = (nlc - 1 - (i - ncc)) if reverse else (i - ncc)
        return jnp.where(in_ctx, b * ncc + c_ctx, lay.bsz * ncc + b * nlc + c_lat)

    return block


def _ssd_scan(lay, xbc, dt, dtb, alog, d_skip, *, heads):
    cch = xbc.shape[1]
    inner = heads * SSM_HEAD_DIM
    assert CHUNK == LANES == SSM_STATE
    fwd, bwd = _chunk_block(lay, False), _chunk_block(lay, True)
    n_steps = lay.ctx_chunks + lay.lat_chunks
    after = lambda order: (lambda b, i: order(b, jnp.minimum(i + 1, n_steps - 1)))
    blk = lambda width, order: pl.BlockSpec((CHUNK, width), lambda b, i: (order(b, i), 0))
    state = pltpu.VMEM((SSM_GROUPS, inner // SSM_GROUPS, SSM_STATE), F32)
    tables = pltpu.VMEM((len(SSD_TABLES), LANES, CHUNK), F32)
    return pl.pallas_call(
        functools.partial(_ssd_kernel, heads=heads),
        grid=(lay.bsz, n_steps),
        in_specs=[blk(cch, fwd), blk(LANES, fwd), blk(LANES, after(fwd)),
                  blk(cch, bwd), blk(LANES, bwd), blk(LANES, after(bwd)),
                  pl.BlockSpec((1, LANES), lambda b, i: (0, 0)),
                  pl.BlockSpec((1, LANES), lambda b, i: (0, 0)),
                  pl.BlockSpec(memory_space=pltpu.SMEM)],
        out_specs=[blk(inner, fwd), blk(inner, bwd)],
        out_shape=[jax.ShapeDtypeStruct((lay.rows, inner), BF16)] * 2,
        scratch_shapes=[state, state, tables, tables],
        compiler_params=_params(2),
        name="ssd_scan",
    )(xbc, dt, dt, xbc, dt, dt, dtb, alog, d_skip)


def _ssm_out_kernel(*refs, lay):
    yf_ref, yb_ref, z_ref, xc_ref, xl_ref, g_ref, nw_ref, w_ref, o_ref = refs
    is_ctx = pl.program_id(0) < lay.ctx_tiles
    inner = z_ref.shape[1]
    acc = None
    sumsq = None
    for (c0, cw) in _col_chunks(0, inner, OUT_COL_CHUNK):
        cs = slice(c0, c0 + cw)
        z = z_ref[:, cs].astype(F32)
        y = (yf_ref[:, cs].astype(F32) + yb_ref[:, cs].astype(F32)) * _silu(z)
        sq = y * y
        part_sq = functools.reduce(lambda u, v: u + v, [sq[:, k:k + LANES] for k in range(0, cw, LANES)])
        sumsq = part_sq if sumsq is None else sumsq + part_sq
        part = _dot((y * nw_ref[:, cs]).astype(BF16), w_ref[cs, :])
        acc = part if acc is None else acc + part
    scale = lax.rsqrt(jnp.sum(sumsq, axis=-1, keepdims=True) * (1.0 / inner) + NORM_EPS)
    resid = jnp.where(is_ctx, xc_ref[...], xl_ref[...])
    o_ref[...] = resid + g_ref[...] * (scale * acc)


def _ssm_out(lay, yf, yb, z, srcs, mods, layer, norm_w, w_out):
    inner = yf.shape[1]
    d = srcs[0].shape[1]
    tm = lay.tm
    tok = lambda c: _rows_spec(tm, c, lambda j: j)
    return pl.pallas_call(
        functools.partial(_ssm_out_kernel, lay=lay), grid=(lay.tiles,),
        in_specs=[tok(inner), tok(inner), tok(inner)]
        + _tok_specs(lay, srcs, tm)
        + [_mod_spec(d, layer, 2, lay.mod_row), _full(norm_w), _layer_spec(w_out, 0)],
        out_specs=tok(d),
        out_shape=jax.ShapeDtypeStruct((lay.rows, d), F32),
        compiler_params=_params(), name="ssm_out",
    )(yf, yb, z, *srcs, mods, norm_w, w_out)


MLSTM_TABLES = ("wm_t", "rowterm", "inter_t", "floor_t", "e_r")
MLSTM_ROWS = ("a_prev", "a_loc", "m_new")


def _mlstm_keep(reverse):
    r_i = lax.broadcasted_iota(jnp.int32, (CHUNK, CHUNK), 0)
    c_i = lax.broadcasted_iota(jnp.int32, (CHUNK, CHUNK), 1)
    return (c_i >= r_i) if reverse else (c_i <= r_i)


def _mlstm_prep(gt_ref, gb_ref, m_prev, *, reverse):
    q = CHUNK
    nh = MLSTM_HEADS
    d = 1 if reverse else 0

    r_i = lax.broadcasted_iota(jnp.int32, (q, q), 0)
    c_i = lax.broadcasted_iota(jnp.int32, (q, q), 1)
    keep = _mlstm_keep(reverse)

    gt = GATE_CAP * jnp.tanh((gt_ref[...] + gb_ref[...]) / GATE_CAP)
    lane_ok = c_i < nh
    ig_sh = (LANES - nh * d) % LANES
    fg_sh = LANES - (2 * nh + nh * d)
    ig_t = gt if ig_sh == 0 else pltpu.roll(gt, ig_sh, 1)
    ig_t = jnp.where(lane_ok, ig_t, 0.0)
    fr_t = pltpu.roll(gt, fg_sh, 1)
    lf_t = jnp.where(lane_ok, -_softplus(-fr_t), 0.0)
    yield
    wv_t = jnp.dot(keep.astype(F32), lf_t, precision=HIGHEST, preferred_element_type=F32)
    ftot = jnp.sum(lf_t, axis=0, keepdims=True)
    yield
    wend_t = ftot - wv_t + ig_t
    mloc = jnp.max(wend_t, axis=0, keepdims=True)
    e_t = jnp.exp(wend_t - mloc)
    m_new = jnp.maximum(ftot + m_prev, mloc)
    a_prev = jnp.exp(ftot + m_prev - m_new)
    a_loc = jnp.exp(mloc - m_new)
    run = ig_t - wv_t
    sh = 1
    while sh < q:
        if reverse:
            moved, ok = pltpu.roll(run, q - sh, 0), r_i < q - sh
        else:
            moved, ok = pltpu.roll(run, sh, 0), r_i >= sh
        run = jnp.maximum(run, jnp.where(ok, moved, -jnp.inf))
        sh *= 2
    yield
    gcol_t = wv_t + m_prev
    mcomb_t = jnp.maximum(gcol_t, wv_t + run)
    wv = wv_t.T
    return dict(wm_t=wv_t - mcomb_t, rowterm=ig_t.T - wv, inter_t=jnp.exp(gcol_t - mcomb_t),
                floor_t=jnp.exp(-mcomb_t), e_r=e_t.T, a_prev=a_prev, a_loc=a_loc, m_new=m_new)


def _mlstm_head(p, qk_ref, v_ref, c_ref, h):
    q = CHUNK
    nh = MLSTM_HEADS
    dk = qk_ref.shape[-1] // (2 * nh)
    dv = v_ref.shape[-1] // nh
    qb = qk_ref[:, h * dk:(h + 1) * dk]
    kt = (qk_ref[:, nh * dk + h * dk: nh * dk + (h + 1) * dk].astype(F32) * (dk ** -0.5)).T
    vaug = jnp.concatenate([v_ref[:, h * dv:(h + 1) * dv], jnp.ones((q, LANES), BF16)], axis=1)
    caug = c_ref[h]
    lanes = lambda col: jnp.broadcast_to(col, (q, LANES))
    pm = jnp.exp(jnp.where(p["keep"], p["wm_t"][:, h:h + 1] + p["rowterm"][h:h + 1, :], -jnp.inf))
    inter = lanes(p["inter_t"][:, h:h + 1])
    floor = lanes(p["floor_t"][:, h:h + 1])
    yield
    s = (_dot(qb, kt.astype(BF16)) * pm).astype(BF16)
    qc = _dot(qb, caug.astype(BF16))
    yield
    sv = _dot(s, vaug)
    ap = p["a_prev"][:, h:h + 1]
    al = p["a_loc"][:, h:h + 1]
    c_new = ap * caug + al * _dot((kt * p["e_r"][h:h + 1, :]).astype(BF16), vaug)
    yield
    blocks = [sv[:, c:c + LANES] + inter * qc[:, c:c + LANES] for c in range(0, dv + LANES, LANES)]
    rden = 1.0 / jnp.maximum(jnp.abs(blocks[-1]), floor)
    h_out = jnp.concatenate([b * rden for b in blocks[:-1]], axis=1)
    return h_out, c_new


def _mlstm_kernel(qkf_ref, vf_ref, gtf_ref, gtf_next_ref, qkb_ref, vb_ref, gtb_ref, gtb_next_ref,
                  gb_ref, hf_ref, hb_ref, cf_ref, tabf_ref, rowf_ref, cb_ref, tabb_ref, rowb_ref):
    def park(tab_ref, row_ref, tables):
        for k, name in enumerate(MLSTM_TABLES):
            tab_ref[k] = tables[name]
        for k, name in enumerate(MLSTM_ROWS):
            row_ref[k] = tables[name]

    @pl.when(pl.program_id(1) == 0)
    def _():
        cf_ref[...] = jnp.zeros_like(cf_ref)
        cb_ref[...] = jnp.zeros_like(cb_ref)
        zero = jnp.zeros((1, LANES), F32)
        first = _lockstep([_mlstm_prep(gtf_ref, gb_ref, zero, reverse=False),
                           _mlstm_prep(gtb_ref, gb_ref, zero, reverse=True)])
        park(tabf_ref, rowf_ref, first[0])
        park(tabb_ref, rowb_ref, first[1])

    dv = vf_ref.shape[-1] // MLSTM_HEADS
    chains = []
    for (tab_ref, row_ref, qk_ref, v_ref, c_ref, h_ref, rev) in (
            (tabf_ref, rowf_ref, qkf_ref, vf_ref, cf_ref, hf_ref, False),
            (tabb_ref, rowb_ref, qkb_ref, vb_ref, cb_ref, hb_ref, True)):
        p = {name: tab_ref[k] for k, name in enumerate(MLSTM_TABLES)}
        p.update({name: row_ref[k] for k, name in enumerate(MLSTM_ROWS)})
        p["keep"] = _mlstm_keep(rev)
        chains.append((p, qk_ref, v_ref, c_ref, h_ref))
    units = [(p, qk_ref, v_ref, c_ref, h_ref, h)
             for (p, qk_ref, v_ref, c_ref, h_ref) in chains for h in range(MLSTM_HEADS)]
    results = _lockstep(
        [_mlstm_head(p, qk_ref, v_ref, c_ref, h) for (p, qk_ref, v_ref, c_ref, _, h) in units]
        + [_mlstm_prep(gtf_next_ref, gb_ref, chains[0][0]["m_new"], reverse=False),
           _mlstm_prep(gtb_next_ref, gb_ref, chains[1][0]["m_new"], reverse=True)])
    for (_, _, _, c_ref, h_ref, h), (h_out, c_new) in zip(units, results):
        h_ref[:, h * dv:(h + 1) * dv] = h_out.astype(h_ref.dtype)
        c_ref[h] = c_new
    park(tabf_ref, rowf_ref, results[-2])
    park(tabb_ref, rowb_ref, results[-1])


def _mlstm_scan(lay, qk, v, gates, gate_b):
    qkw, vw = qk.shape[1], v.shape[1]
    nh = MLSTM_HEADS
    dk, dv = qkw // (2 * nh), vw // nh
    ncc, nlc = lay.ctx_chunks, lay.lat_chunks
    fwd, bwd = _chunk_block(lay, False), _chunk_block(lay, True)
    out_f = lambda b, i: b * nlc + jnp.maximum(i - ncc, 0)
    out_b = lambda b, i: b * nlc + nlc - 1 - jnp.maximum(i - ncc, 0)
    assert CHUNK == LANES
    n_steps = ncc + nlc
    after = lambda order: (lambda b, i: order(b, jnp.minimum(i + 1, n_steps - 1)))
    blk = lambda width, order: pl.BlockSpec((CHUNK, width), lambda b, i: (order(b, i), 0))
    ins = lambda order: [blk(qkw, order), blk(vw, order), blk(LANES, order), blk(LANES, after(order))]
    state = [pltpu.VMEM((nh, dk, dv + LANES), F32),
             pltpu.VMEM((len(MLSTM_TABLES), CHUNK, LANES), F32),
             pltpu.VMEM((len(MLSTM_ROWS), 1, LANES), F32)]
    return pl.pallas_call(
        _mlstm_kernel,
        grid=(lay.bsz, n_steps),
        in_specs=ins(fwd) + ins(bwd) + [pl.BlockSpec((1, LANES), lambda b, i: (0, 0))],
        out_specs=[blk(vw, out_f), blk(vw, out_b)],
        out_shape=[jax.ShapeDtypeStruct((lay.lat_rows, vw), BF16)] * 2,
        scratch_shapes=state + state,
        compiler_params=_params(2),
        name="mlstm_scan",
    )(qk, v, gates, gates, qk, v, gates, gates, gate_b)


def _mlstm_out_up_kernel(hf_ref, hb_ref, o_ref_in, x_ref, g_ref, nw_ref, w_ref,
                         n2_ref, sh_ref, sc_ref, wup_ref, x_out_ref, a_ref, gl_ref):
    nh = MLSTM_HEADS
    dv = hf_ref.shape[-1] // nh
    for r0 in range(0, hf_ref.shape[0], OUT_ROW_BLOCK):
        rows = slice(r0, r0 + OUT_ROW_BLOCK)
        parts = []
        for h in range(nh):
            hh = (hf_ref[rows, h * dv:(h + 1) * dv].astype(F32)
                  + hb_ref[rows, h * dv:(h + 1) * dv].astype(F32))
            parts.append(_rms_scale(hh))
        hn = jnp.concatenate(parts, axis=1)
        y = (hn * nw_ref[...] * jax.nn.sigmoid(o_ref_in[rows, :].astype(F32))).astype(BF16)
        x_out_ref[rows, :] = x_ref[rows, :] + g_ref[...] * _dot(y, w_ref[...])
    h2 = _rms_scale(x_out_ref[...]) * n2_ref[...]
    hb2 = (h2 * (1.0 + sc_ref[...]) + sh_ref[...]).astype(BF16)
    f = a_ref.shape[1]
    for out_ref, s0 in ((a_ref, 0), (gl_ref, f)):
        for (c0, cw) in _col_chunks(s0, f):
            out_ref[:, c0 - s0:c0 - s0 + cw] = _dot(hb2, wup_ref[:, c0:c0 + cw]).astype(out_ref.dtype)


def _mlstm_out_up(lay, hf, hb, o_gate, xall, mods, layer, norm_w, w_out, norm2_w, w_up):
    vw = hf.shape[1]
    d = xall.shape[1]
    f = w_up.shape[2] // 2
    tm = lay.tm
    lat = lambda c: _rows_spec(tm, c, lambda j: j)
    allt = lambda c: _rows_spec(tm, c, lambda j: j + lay.ctx_tiles)
    row_fn = lambda j: j // lay.lat_tiles_per_batch
    return pl.pallas_call(
        _mlstm_out_up_kernel, grid=(lay.lat_tiles,),
        in_specs=[lat(vw), lat(vw), allt(vw), allt(d),
                  _mod_spec(d, layer, 2, row_fn), _full(norm_w), _layer_spec(w_out, 0),
                  _full(norm2_w), _mod_spec(d, layer, 3, row_fn), _mod_spec(d, layer, 4, row_fn),
                  _layer_spec(w_up, layer)],
        out_specs=[lat(d), lat(f), lat(f)],
        out_shape=[jax.ShapeDtypeStruct((lay.lat_rows, d), F32),
                   jax.ShapeDtypeStruct((lay.lat_rows, f), F32),
                   jax.ShapeDtypeStruct((lay.lat_rows, f), BF16)],
        compiler_params=_params(), name="mlstm_out_ffn_up",
    )(hf, hb, o_gate, xall, mods, norm_w, w_out, norm2_w, mods, mods, w_up)


def _ffn_down_kernel(*refs, lay, skip, final_norm):
    if final_norm:
        a_ref, ap_ref, an_ref, gl_ref, x_ref, g_ref, cw_ref, cb_ref, w_ref, fw_ref, o_ref = refs
    else:
        a_ref, ap_ref, an_ref, gl_ref, x_ref, g_ref, cw_ref, cb_ref, w_ref, o_ref = refs
    tm, f = a_ref.shape
    gw = GRID_WIDTH
    j = pl.program_id(0) + skip
    is_ctx = j < lay.ctx_tiles
    pos = lay.batch_pos(j)
    has_prev = jnp.logical_and(jnp.logical_not(is_ctx), pos > 0)
    has_next = jnp.logical_and(jnp.logical_not(is_ctx), pos < lay.lat_tiles_per_batch - 1)

    cw = FFN_COL_CHUNK
    vert = jnp.where(is_ctx, 0.0, 1.0)
    sb = FFN_SHIFT_BLOCK
    ri = lax.broadcasted_iota(jnp.int32, (sb, sb), 0)
    ci = lax.broadcasted_iota(jnp.int32, (sb, sb), 1)
    colpos = jnp.where(is_ctx, ri % lay.n_ctx, ri & (gw - 1))
    last_col = jnp.where(is_ctx, lay.n_ctx - 1, gw - 1)
    take_left = jnp.where(colpos != 0, jnp.where(ci == ri - 1, 1.0, 0.0), 0.0)
    take_right = jnp.where(colpos != last_col, jnp.where(ci == ri + 1, 1.0, 0.0), 0.0)
    shift = jnp.concatenate([take_left, take_right], axis=1).astype(BF16)

    def taps(c0):
        cs = slice(c0, c0 + cw)
        up = jnp.where(has_prev, ap_ref[:, cs], 0.0)
        dn = jnp.where(has_next, an_ref[:, cs], 0.0)
        slabs = [jnp.concatenate([up, a_ref[0:tm - gw, cs]], axis=0), a_ref[:, cs],
                 jnp.concatenate([a_ref[gw:tm, cs], dn], axis=0)]

        def vsum(dj, parts, dtype):
            k = lambda di: 3 * (di + 1) + (dj + 1)
            return (parts[0] * (cw_ref[k(-1):k(-1) + 1, cs] * vert).astype(dtype)
                    + parts[1] * cw_ref[k(0):k(0) + 1, cs].astype(dtype)
                    + parts[2] * (cw_ref[k(1):k(1) + 1, cs] * vert).astype(dtype))

        slabs_b = [s.astype(BF16) for s in slabs]
        left, right = vsum(-1, slabs_b, BF16), vsum(1, slabs_b, BF16)
        sides = jnp.concatenate(
            [_dot(shift, jnp.concatenate([left[r0:r0 + sb], right[r0:r0 + sb]], axis=0))
             for r0 in range(0, tm, sb)], axis=0)
        return vsum(0, slabs, F32) + cb_ref[:, cs], sides

    def finish(c0, centre, sides, acc_out):
        cs = slice(c0, c0 + cw)
        mid = (_silu(centre + sides) * gl_ref[:, cs].astype(F32)).astype(BF16)
        part = _dot(mid, w_ref[cs, :])
        return part if acc_out is None else acc_out + part

    acc_out = None
    pending = None
    for c0 in range(0, f, cw):
        cur = (c0,) + taps(c0)
        if pending is not None:
            acc_out = finish(*pending, acc_out)
        pending = cur
    acc_out = finish(*pending, acc_out)
    out = x_ref[...] + g_ref[...] * acc_out
    if final_norm:
        out = _rms_scale(out) * fw_ref[...]
    o_ref[...] = out


def _ffn_down(lay, a, gl, xres, mods, layer, conv_w9, conv_b, w_down, *, skip=0, final_w=None):
    tt, f = a.shape
    d = xres.shape[1]
    tm = lay.tm
    gw = GRID_WIDTH
    assert f % FFN_COL_CHUNK == 0 and tm % FFN_SHIFT_BLOCK == 0
    assert FFN_SHIFT_BLOCK % gw == 0 and FFN_SHIFT_BLOCK % lay.n_ctx == 0
    n_tiles = tt // tm
    hb = tm // gw
    tok = lambda c: _rows_spec(tm, c, lambda j: j)
    in_specs = [
        tok(f),
        _rows_spec(gw, f, lambda j: jnp.maximum(j * hb - 1, 0)),
        _rows_spec(gw, f, lambda j: jnp.minimum((j + 1) * hb, tt // gw - 1)),
        tok(f), tok(d), _mod_spec(d, layer, 5, lambda j: lay.mod_row(j + skip)),
        _full(conv_w9), _full(conv_b), _layer_spec(w_down, layer)]
    args = [a, a, a, gl, xres, mods, conv_w9, conv_b, w_down]
    if final_w is not None:
        in_specs.append(_full(final_w))
        args.append(final_w)
    kern = functools.partial(_ffn_down_kernel, lay=lay, skip=skip, final_norm=final_w is not None)
    return pl.pallas_call(
        kern, grid=(n_tiles,), in_specs=in_specs, out_specs=tok(d),
        out_shape=jax.ShapeDtypeStruct((tt, d), F32),
        compiler_params=_params(), name=f"ffn_down_l{layer}",
    )(*args)


def _pad_cols(a, n):
    return jnp.pad(a, [(0, 0)] * (a.ndim - 1) + [(0, n - a.shape[-1])])


def kernel(x, c, ctx, c_ctx, ada_w, ada_b, norm1_w, norm2_w, ssm_w_in, ssm_conv_w, ssm_conv_b, ssm_dt_bias, ssm_a_log, ssm_d, ssm_norm_w, ssm_w_out, mlstm_w_in, mlstm_conv_w, mlstm_conv_b, mlstm_gate_b, mlstm_norm_w, mlstm_w_out, ffn_w_up, ffn_conv_w, ffn_conv_b, ffn_w_down, final_norm_w):
    bsz, seq, d = x.shape
    n_ctx = ctx.shape[1]
    depth = ada_w.shape[0]
    assert depth == 2 and ssm_w_in.shape[0] == 1 and mlstm_w_in.shape[0] == 1
    lay = _Layout(bsz, n_ctx, seq)
    x_flat = x.reshape(bsz * seq, d)
    ctx_flat = ctx.reshape(bsz * n_ctx, d)

    n_rows = -(-(bsz + 1) // SUBLANES) * SUBLANES
    cvec = jnp.zeros((n_rows, d), F32).at[:bsz].set(c).at[bsz].set(c_ctx)
    mods = _ada(cvec, ada_w, ada_b).reshape(depth, n_rows, 6, 1, d)

    w_up = ffn_w_up.astype(BF16)
    w_down = ffn_w_down.astype(BF16)
    f = ffn_conv_b.shape[1]
    conv9 = ffn_conv_w.reshape(depth, 9, f)

    heads = ssm_d.shape[1]
    assert 2 * heads <= LANES
    inner = heads * SSM_HEAD_DIM
    bc_w = 2 * SSM_GROUPS * SSM_STATE
    conv_ch = inner + bc_w
    w_in = _pad_cols(ssm_w_in, 2 * inner + bc_w + LANES).astype(BF16)
    splits = [(0, inner), (inner, conv_ch), (inner + conv_ch, LANES)]
    z, xbc, dt = _project(
        lay, (ctx_flat, x_flat), norm1_w[0:1], mods, 0, 0, 1, w_in, splits, (BF16, BF16, F32),
        conv=(1, ssm_conv_w[0], ssm_conv_b[0:1]), name="in_proj_l0")
    dtb = _pad_cols(ssm_dt_bias[0].reshape(1, -1), LANES)
    alog = _pad_cols(ssm_a_log[0].reshape(1, -1), LANES)
    y_f, y_b = _ssd_scan(lay, xbc, dt, dtb, alog, ssm_d[0], heads=heads)
    xall = _ssm_out(lay, y_f, y_b, z, (ctx_flat, x_flat), mods, 0, ssm_norm_w[0:1],
                    ssm_w_out.astype(BF16))

    a, gl = _project(lay, (xall,), norm2_w[0:1], mods, 0, 3, 4, w_up, [(0, f), (f, f)], (F32, BF16),
                     name="ffn_up_l0")
    xall = _ffn_down(lay, a, gl, xall, mods, 0, conv9[0], ffn_conv_b[0:1], w_down)

    qkw = mlstm_conv_b.shape[1]
    vw = mlstm_norm_w.shape[1]
    w_in = _pad_cols(mlstm_w_in, qkw + 2 * vw + LANES).astype(BF16)
    splits = [(0, qkw), (qkw, vw), (qkw + vw, vw), (qkw + 2 * vw, LANES)]
    qk, v, o_gate, gates = _project(
        lay, (xall,), norm1_w[1:2], mods, 1, 0, 1, w_in, splits, (BF16, BF16, BF16, F32),
        conv=(0, mlstm_conv_w[0], mlstm_conv_b[0:1]), name="in_proj_l1")
    gate_b = _pad_cols(mlstm_gate_b[0:1], LANES)
    h_f, h_b = _mlstm_scan(lay, qk, v, gates, gate_b)
    xlat, a, gl = _mlstm_out_up(lay, h_f, h_b, o_gate, xall, mods, 1, mlstm_norm_w[0:1],
                                mlstm_w_out.astype(BF16), norm2_w[1:2], w_up)
    out = _ffn_down(lay, a, gl, xlat, mods, 1, conv9[1], ffn_conv_b[1:2], w_down,
                    skip=lay.ctx_tiles, final_w=final_norm_w.reshape(1, d))
    return out.reshape(bsz, seq, d)
```

```python
import functools

import jax
import jax.numpy as jnp
from jax import lax
from jax.experimental import pallas as pl
from jax.experimental.pallas import tpu as pltpu

F32 = jnp.float32
BF16 = jnp.bfloat16
HIGHEST = lax.Precision.HIGHEST

NORM_EPS = 1e-6
GRID_WIDTH = 64
SSM_HEAD_DIM = 64
SSM_GROUPS = 4
SSM_STATE = 128
MLSTM_HEADS = 4
GATE_CAP = 15.0
CHUNK = 128

LANES = 128
SUBLANES = 8
TOKEN_TILE = 512
COL_CHUNK = 512
FFN_COL_CHUNK = 256
FFN_SHIFT_BLOCK = 256
SCAN_SUBSTEPS = 2
OUT_ROW_BLOCK = 256
OUT_COL_CHUNK = 256
VMEM_LIMIT = 56 * 1024 * 1024


class _Layout:
    def __init__(self, bsz, n_ctx, seq, tm=TOKEN_TILE):
        assert tm % n_ctx == 0 and (bsz * n_ctx) % tm == 0 and seq % tm == 0
        assert n_ctx % CHUNK == 0 and seq % CHUNK == 0 and seq % GRID_WIDTH == 0
        self.bsz, self.n_ctx, self.seq, self.tm = bsz, n_ctx, seq, tm
        self.ctx_rows, self.lat_rows = bsz * n_ctx, bsz * seq
        self.rows = self.ctx_rows + self.lat_rows
        self.ctx_tiles = self.ctx_rows // tm
        self.lat_tiles_per_batch = seq // tm
        self.lat_tiles = bsz * self.lat_tiles_per_batch
        self.tiles = self.ctx_tiles + self.lat_tiles
        self.ctx_chunks, self.lat_chunks = n_ctx // CHUNK, seq // CHUNK

    def mod_row(self, j):
        return jnp.where(j < self.ctx_tiles, self.bsz, (j - self.ctx_tiles) // self.lat_tiles_per_batch)

    def batch_pos(self, j):
        return (j - self.ctx_tiles) % self.lat_tiles_per_batch


def _params(n_axes=1):
    return pltpu.CompilerParams(
        dimension_semantics=("arbitrary",) * n_axes, vmem_limit_bytes=VMEM_LIMIT)


def _col_chunks(start, width, step=COL_CHUNK):
    out, c = [], start
    while c < start + width:
        w = min(step, start + width - c)
        out.append((c, w))
        c += w
    return out


def _silu(v):
    half = 0.5 * v
    return half + half * jnp.tanh(half)


def _softplus(v):
    return jnp.maximum(v, 0.0) + jnp.log1p(jnp.exp(-jnp.abs(v)))


def _rms_scale(v):
    return v * lax.rsqrt(jnp.mean(v * v, axis=-1, keepdims=True) + NORM_EPS)


def _dot(a, b):
    return jnp.dot(a, b, preferred_element_type=F32)


def _dot_nt(a, b):
    return lax.dot_general(a, b, (((1,), (1,)), ((), ())), preferred_element_type=F32)


def _lockstep_inner(gens):
    results = [None] * len(gens)
    live = list(range(len(gens)))
    while live:
        for i in list(live):
            try:
                next(gens[i])
            except StopIteration as done:
                results[i] = done.value
                live.remove(i)
        yield
    return results


def _lockstep(gens):
    it = _lockstep_inner(gens)
    while True:
        try:
            next(it)
        except StopIteration as done:
            return done.value


def _full(a):
    return pl.BlockSpec(a.shape, lambda *_: (0,) * a.ndim)


def _layer_spec(a, layer):
    return pl.BlockSpec((None,) + a.shape[1:], lambda *_: (layer,) + (0,) * (a.ndim - 1),
                        pipeline_mode=pl.Buffered(1))


def _mod_spec(d, layer, k, row_fn):
    return pl.BlockSpec((None, None, None, 1, d), lambda j: (layer, row_fn(j), k, 0, 0))


def _rows_spec(rows, width, idx_fn):
    return pl.BlockSpec((rows, width), lambda j: (idx_fn(j), 0))


def _ada_kernel(c_ref, w_ref, b_ref, o_ref):
    s = _silu(c_ref[...])
    o_ref[...] = jnp.dot(s, w_ref[...], precision=HIGHEST, preferred_element_type=F32) + b_ref[...]


def _ada(cvec, ada_w, ada_b):
    depth, d, n = ada_w.shape
    rows = cvec.shape[0]
    tn = 512
    return pl.pallas_call(
        _ada_kernel,
        grid=(depth, n // tn),
        in_specs=[
            pl.BlockSpec((rows, d), lambda l, j: (0, 0)),
            pl.BlockSpec((None, d, tn), lambda l, j: (l, 0, j)),
            pl.BlockSpec((None, 1, tn), lambda l, j: (l, 0, j)),
        ],
        out_specs=pl.BlockSpec((None, rows, tn), lambda l, j: (l, 0, j)),
        out_shape=jax.ShapeDtypeStruct((depth, rows, n), F32),
        compiler_params=_params(2),
        name="ada_mod",
    )(cvec, ada_w, ada_b.reshape(depth, 1, n))


def _tok_specs(lay, arrs, rows, shift=0):
    per = lay.tm // rows
    if len(arrs) == 1:
        (a,) = arrs
        hi = a.shape[0] // rows - 1
        return [_rows_spec(rows, a.shape[1], lambda j: jnp.clip(j * per + shift, 0, hi))]
    ctx, lat = arrs
    hi_c = ctx.shape[0] // rows - 1
    hi_l = lat.shape[0] // rows - 1
    return [_rows_spec(rows, ctx.shape[1], lambda j: jnp.clip(j * per + shift, 0, hi_c)),
            _rows_spec(rows, lat.shape[1],
                       lambda j: jnp.clip((j - lay.ctx_tiles) * per + shift, 0, hi_l))]


def _tok_read(refs, is_ctx):
    if len(refs) == 1:
        return refs[0][...]
    return jnp.where(is_ctx, refs[0][...], refs[1][...])


def _proj_kernel(*refs, lay, n_src, splits, conv_idx, skip):
    refs = list(refs)
    x_refs = [refs.pop(0) for _ in range(n_src)]
    if conv_idx is not None:
        xp_refs = [refs.pop(0) for _ in range(n_src)]
        xn_refs = [refs.pop(0) for _ in range(n_src)]
    nw_ref, sh_ref, sc_ref, w_ref = refs[:4]
    refs = refs[4:]
    if conv_idx is not None:
        cw_ref, cb_ref = refs[:2]
        refs = refs[2:]
    outs = refs
    j = pl.program_id(0) + skip
    is_ctx = j < lay.ctx_tiles
    tm = lay.tm

    def prep(v):
        h = _rms_scale(v) * nw_ref[...]
        return h * (1.0 + sc_ref[...]) + sh_ref[...]

    h_main = prep(_tok_read(x_refs, is_ctx))
    hb = h_main.astype(BF16)
    if conv_idx is not None:
        pos = lay.batch_pos(j)
        is_first = jnp.logical_or(is_ctx, pos == 0)
        is_last = jnp.logical_or(is_ctx, pos == lay.lat_tiles_per_batch - 1)
        hp = jnp.where(is_first, 0.0, prep(_tok_read(xp_refs, is_ctx)))
        hn = jnp.where(is_last, 0.0, prep(_tok_read(xn_refs, is_ctx)))
        hext = jnp.concatenate([hp, h_main, hn], axis=0).astype(BF16)
        te = tm + 2 * SUBLANES
        seam_rows = lax.broadcasted_iota(jnp.int32, (te, 1), 0) - SUBLANES
        keep_up = jnp.where(jnp.logical_and(is_ctx, seam_rows % lay.n_ctx == 0), 0.0, 1.0)
        keep_dn = jnp.where(jnp.logical_and(is_ctx, seam_rows % lay.n_ctx == lay.n_ctx - 1), 0.0, 1.0)
    for oi, (s0, sw) in enumerate(splits):
        for (c0, cw) in _col_chunks(s0, sw):
            if oi == conv_idx:
                r = _dot(hext, w_ref[:, c0:c0 + cw])
                k0 = c0 - s0

                def conv(up, dn):
                    y = (up * cw_ref[0:1, k0:k0 + cw] + r * cw_ref[1:2, k0:k0 + cw]
                         + dn * cw_ref[2:3, k0:k0 + cw] + cb_ref[:, k0:k0 + cw])
                    return _silu(y[SUBLANES:SUBLANES + tm, :])

                up, dn = pltpu.roll(r, 1, 0), pltpu.roll(r, te - 1, 0)
                r = conv(up * keep_up, dn * keep_dn)
            else:
                r = _dot(hb, w_ref[:, c0:c0 + cw])
            outs[oi][:, c0 - s0:c0 - s0 + cw] = r.astype(outs[oi].dtype)


def _project(lay, srcs, norm_w, mods, layer, k_shift, k_scale, w, splits, out_dtypes, *,
             conv=None, skip=0, name):
    d = srcs[0].shape[1]
    tm = lay.tm
    n_tiles = lay.tiles - skip
    row_fn = lambda j: lay.mod_row(j + skip)
    in_specs = _tok_specs(lay, srcs, tm)
    args = list(srcs)
    conv_idx = None
    if conv is not None:
        assert skip == 0
        conv_idx, conv_w, conv_b = conv
        in_specs += _tok_specs(lay, srcs, SUBLANES, shift=-1)
        in_specs += _tok_specs(lay, srcs, SUBLANES, shift=tm // SUBLANES)
        args += list(srcs) * 2
    in_specs += [_full(norm_w), _mod_spec(d, layer, k_shift, row_fn),
                 _mod_spec(d, layer, k_scale, row_fn), _layer_spec(w, layer if w.shape[0] > 1 else 0)]
    args += [norm_w, mods, mods, w]
    if conv is not None:
        in_specs += [_full(conv_w), _full(conv_b)]
        args += [conv_w, conv_b]
    out_specs = [_rows_spec(tm, sw, lambda j: j) for (_, sw) in splits]
    out_shape = [jax.ShapeDtypeStruct((n_tiles * tm, sw), dt) for (_, sw), dt in zip(splits, out_dtypes)]
    kern = functools.partial(_proj_kernel, lay=lay, n_src=len(srcs), splits=tuple(splits),
                             conv_idx=conv_idx, skip=skip)
    return pl.pallas_call(
        kern, grid=(n_tiles,), in_specs=in_specs, out_specs=out_specs, out_shape=out_shape,
        compiler_params=_params(), name=name,
    )(*args)


SSD_TABLES = ("w", "w_t", "dts", "dte", "ew", "etot")


def _ssd_prep(dt_ref, dtb_ref, alog_ref, *, reverse):
    q = CHUNK
    r_i = lax.broadcasted_iota(jnp.int32, (q, q), 0)
    c_i = lax.broadcasted_iota(jnp.int32, (q, q), 1)
    tri = (c_i >= r_i) if reverse else (c_i <= r_i)
    dts_t = _softplus(dt_ref[...] + dtb_ref[...])
    la_t = dts_t * (-jnp.exp(alog_ref[...]))
    yield
    w_t = jnp.dot(tri.astype(F32), la_t, precision=HIGHEST, preferred_element_type=F32)
    yield
    w = w_t.T
    dts = dts_t.T
    yield
    tot = w[:, 0:1] if reverse else w[:, q - 1:q]
    dte = jnp.exp(tot - w)
    ew = jnp.exp(w)
    etot = jnp.broadcast_to(jnp.exp(tot), (LANES, SSM_STATE))
    return dict(w=w, w_t=w_t, dts=dts, dte=dte, ew=ew, etot=etot)


def _ssd_mask(reverse):
    r_i = lax.broadcasted_iota(jnp.int32, (CHUNK, CHUNK), 0)
    c_i = lax.broadcasted_iota(jnp.int32, (CHUNK, CHUNK), 1)
    return (r_i >= c_i) if reverse else (r_i <= c_i)


def _ssd_group(p, xbc_ref, st_ref, g, *, reverse, heads, skip_ref=None):
    hd, ng, ns = SSM_HEAD_DIM, SSM_GROUPS, SSM_STATE
    hpg = heads // ng
    inner = heads * hd
    dir_off = heads if reverse else 0
    w, w_t = p["w"], p["w_t"]
    bb = xbc_ref[:, inner + g * ns: inner + (g + 1) * ns]
    cb = xbc_ref[:, inner + ng * ns + g * ns: inner + ng * ns + (g + 1) * ns]
    cbt = _dot_nt(bb, cb)
    st_g = st_ref[g]
    yint = _dot_nt(st_g.astype(BF16), cb)
    y_parts, xdec_parts = [], []
    for pr in range(hpg // 2):
        col = (g * hpg + 2 * pr) * hd
        blk_t = xbc_ref[:, col:col + 2 * hd].astype(F32).T
        for half in range(2):
            r = 2 * pr + half
            hh = dir_off + g * hpg + r
            xd = blk_t[half * hd:(half + 1) * hd, :] * p["dts"][hh:hh + 1, :]
            expo = jnp.where(p["mask_sl"], w[hh:hh + 1, :] - w_t[:, hh:hh + 1], -jnp.inf)
            mt = (cbt * jnp.exp(expo)).astype(BF16)
            y_h = (_dot(xd.astype(BF16), mt)
                   + yint[r * hd:(r + 1) * hd, :] * p["ew"][hh:hh + 1, :])
            if skip_ref is not None:
                y_h = y_h + skip_ref[g * hpg + r] * blk_t[half * hd:(half + 1) * hd, :]
            y_parts.append(y_h)
            xdec_parts.append((xd * p["dte"][hh:hh + 1, :]).astype(BF16))
    xdec = jnp.concatenate(xdec_parts, axis=0)
    sloc = _dot(xdec, bb)
    new_state = [st_g[r * hd:(r + 1) * hd, :] * p["etot"][dir_off + g * hpg + r:dir_off + g * hpg + r + 1, :]
                 + sloc[r * hd:(r + 1) * hd, :] for r in range(hpg)]
    y_blocks = [jnp.concatenate(y_parts[2 * pr:2 * pr + 2], axis=0).T for pr in range(hpg // 2)]
    return y_blocks, new_state


def _sub_rows(ref, s, reverse):
    idx = (SCAN_SUBSTEPS - 1 - s) if reverse else s
    return ref.at[pl.ds(idx * CHUNK, CHUNK), :]


def _ssd_kernel(xf_ref, dtf_ref, dtf_next_ref, xb_ref, dtb2_ref, dtb_next_ref, dtb_ref, alog_ref,
                d_ref, yf_ref, yb_ref, stf_ref, stb_ref, tabf_ref, tabb_ref, *, heads):
    def park(tab_ref, tables):
        for k, name in enumerate(SSD_TABLES):
            tab_ref[k] = tables[name]

    @pl.when(pl.program_id(1) == 0)
    def _():
        stf_ref[...] = jnp.zeros_like(stf_ref)
        stb_ref[...] = jnp.zeros_like(stb_ref)
        first = _lockstep([_ssd_prep(_sub_rows(dtf_ref, 0, False), dtb_ref, alog_ref, reverse=False),
                           _ssd_prep(_sub_rows(dtb2_ref, 0, True), dtb_ref, alog_ref, reverse=True)])
        park(tabf_ref, first[0])
        park(tabb_ref, first[1])

    for s in range(SCAN_SUBSTEPS):
        last = s == SCAN_SUBSTEPS - 1
        _ssd_substep(
            _sub_rows(xf_ref, s, False),
            _sub_rows(dtf_next_ref, 0, False) if last else _sub_rows(dtf_ref, s + 1, False),
            _sub_rows(xb_ref, s, True),
            _sub_rows(dtb_next_ref, 0, True) if last else _sub_rows(dtb2_ref, s + 1, True),
            dtb_ref, alog_ref, d_ref, _sub_rows(yf_ref, s, False), _sub_rows(yb_ref, s, True),
            stf_ref, stb_ref, tabf_ref, tabb_ref, park, heads=heads)


def _ssd_substep(xf_ref, dtf_next_ref, xb_ref, dtb_next_ref, dtb_ref, alog_ref, d_ref, yf_ref, yb_ref,
                 stf_ref, stb_ref, tabf_ref, tabb_ref, park, *, heads):
    hd = SSM_HEAD_DIM
    hpg = heads // SSM_GROUPS
    chains = []
    for (tab_ref, x_ref, st_ref, y_ref, rev) in ((tabf_ref, xf_ref, stf_ref, yf_ref, False),
                                                 (tabb_ref, xb_ref, stb_ref, yb_ref, True)):
        p = {name: tab_ref.at[k] for k, name in enumerate(SSD_TABLES)}
        p["mask_sl"] = _ssd_mask(rev)
        chains.append((p, x_ref, st_ref, y_ref, rev))
    upcoming = _lockstep_inner([_ssd_prep(dtf_next_ref, dtb_ref, alog_ref, reverse=False),
                                _ssd_prep(dtb_next_ref, dtb_ref, alog_ref, reverse=True)])
    nxt = None
    for g in range(SSM_GROUPS):
        results = [_ssd_group(p, x_ref, st_ref, g, reverse=rev, heads=heads,
                              skip_ref=None if rev else d_ref)
                   for (p, x_ref, st_ref, _, rev) in chains]
        if nxt is None:
            try:
                next(upcoming)
            except StopIteration as done:
                nxt = done.value
        for (_, _, st_ref, y_ref, _), (y_blocks, new_state) in zip(chains, results):
            for r, s_new in enumerate(new_state):
                st_ref[g, r * hd:(r + 1) * hd, :] = s_new
            for pr, blk in enumerate(y_blocks):
                col = (g * hpg + 2 * pr) * hd
                y_ref[:, col:col + 2 * hd] = blk.astype(y_ref.dtype)
    while nxt is None:
        try:
            next(upcoming)
        except StopIteration as done:
            nxt = done.value
    park(tabf_ref, nxt[0])
    park(tabb_ref, nxt[1])


def _chunk_block(lay, reverse):
    assert lay.ctx_chunks % SCAN_SUBSTEPS == 0 and lay.lat_chunks % SCAN_SUBSTEPS == 0
    ncc, nlc = lay.ctx_chunks // SCAN_SUBSTEPS, lay.lat_chunks // SCAN_SUBSTEPS

    def block(b, i):
        in_ctx = i < ncc
        c_ctx = (ncc - 1 - i) if reverse else i
        c_lat = (nlc - 1 - (i - ncc)) if reverse else (i - ncc)
        return jnp.where(in_ctx, b * ncc + c_ctx, lay.bsz * ncc + b * nlc + c_lat)

    return block


def _ssd_scan(lay, xbc, dt, dtb, alog, d_skip, *, heads):
    cch = xbc.shape[1]
    inner = heads * SSM_HEAD_DIM
    assert CHUNK == LANES == SSM_STATE
    fwd, bwd = _chunk_block(lay, False), _chunk_block(lay, True)
    n_steps = (lay.ctx_chunks + lay.lat_chunks) // SCAN_SUBSTEPS
    after = lambda order: (lambda b, i: order(b, jnp.minimum(i + 1, n_steps - 1)))
    blk = lambda width, order: pl.BlockSpec((SCAN_SUBSTEPS * CHUNK, width),
                                            lambda b, i: (order(b, i), 0))
    state = pltpu.VMEM((SSM_GROUPS, inner // SSM_GROUPS, SSM_STATE), F32)
    tables = pltpu.VMEM((len(SSD_TABLES), LANES, CHUNK), F32)
    return pl.pallas_call(
        functools.partial(_ssd_kernel, heads=heads),
        grid=(lay.bsz, n_steps),
        in_specs=[blk(cch, fwd), blk(LANES, fwd), blk(LANES, after(fwd)),
                  blk(cch, bwd), blk(LANES, bwd), blk(LANES, after(bwd)),
                  pl.BlockSpec((1, LANES), lambda b, i: (0, 0)),
                  pl.BlockSpec((1, LANES), lambda b, i: (0, 0)),
                  pl.BlockSpec(memory_space=pltpu.SMEM)],
        out_specs=[blk(inner, fwd), blk(inner, bwd)],
        out_shape=[jax.ShapeDtypeStruct((lay.rows, inner), BF16)] * 2,
        scratch_shapes=[state, state, tables, tables],
        compiler_params=_params(2),
        name="ssd_scan",
    )(xbc, dt, dt, xbc, dt, dt, dtb, alog, d_skip)


def _ssm_out_kernel(*refs, lay):
    yf_ref, yb_ref, z_ref, xc_ref, xl_ref, g_ref, nw_ref, w_ref, o_ref = refs
    is_ctx = pl.program_id(0) < lay.ctx_tiles
    inner = z_ref.shape[1]
    acc = None
    sumsq = None
    for (c0, cw) in _col_chunks(0, inner, OUT_COL_CHUNK):
        cs = slice(c0, c0 + cw)
        z = z_ref[:, cs].astype(F32)
        y = (yf_ref[:, cs].astype(F32) + yb_ref[:, cs].astype(F32)) * _silu(z)
        sq = y * y
        part_sq = functools.reduce(lambda u, v: u + v, [sq[:, k:k + LANES] for k in range(0, cw, LANES)])
        sumsq = part_sq if sumsq is None else sumsq + part_sq
        part = _dot((y * nw_ref[:, cs]).astype(BF16), w_ref[cs, :])
        acc = part if acc is None else acc + part
    scale = lax.rsqrt(jnp.sum(sumsq, axis=-1, keepdims=True) * (1.0 / inner) + NORM_EPS)
    resid = jnp.where(is_ctx, xc_ref[...], xl_ref[...])
    o_ref[...] = resid + g_ref[...] * (scale * acc)


def _ssm_out(lay, yf, yb, z, srcs, mods, layer, norm_w, w_out):
    inner = yf.shape[1]
    d = srcs[0].shape[1]
    tm = lay.tm
    tok = lambda c: _rows_spec(tm, c, lambda j: j)
    return pl.pallas_call(
        functools.partial(_ssm_out_kernel, lay=lay), grid=(lay.tiles,),
        in_specs=[tok(inner), tok(inner), tok(inner)]
        + _tok_specs(lay, srcs, tm)
        + [_mod_spec(d, layer, 2, lay.mod_row), _full(norm_w), _layer_spec(w_out, 0)],
        out_specs=tok(d),
        out_shape=jax.ShapeDtypeStruct((lay.rows, d), F32),
        compiler_params=_params(), name="ssm_out",
    )(yf, yb, z, *srcs, mods, norm_w, w_out)


MLSTM_TABLES = ("wm_t", "rowterm", "inter_t", "floor_t", "e_r")
MLSTM_ROWS = ("a_prev", "a_loc", "m_new")


def _mlstm_keep(reverse):
    r_i = lax.broadcasted_iota(jnp.int32, (CHUNK, CHUNK), 0)
    c_i = lax.broadcasted_iota(jnp.int32, (CHUNK, CHUNK), 1)
    return (c_i >= r_i) if reverse else (c_i <= r_i)


def _mlstm_prep(gt_ref, gb_ref, m_prev, *, reverse):
    q = CHUNK
    nh = MLSTM_HEADS
    d = 1 if reverse else 0

    r_i = lax.broadcasted_iota(jnp.int32, (q, q), 0)
    c_i = lax.broadcasted_iota(jnp.int32, (q, q), 1)
    keep = _mlstm_keep(reverse)

    gt = GATE_CAP * jnp.tanh((gt_ref[...] + gb_ref[...]) / GATE_CAP)
    lane_ok = c_i < nh
    ig_sh = (LANES - nh * d) % LANES
    fg_sh = LANES - (2 * nh + nh * d)
    ig_t = gt if ig_sh == 0 else pltpu.roll(gt, ig_sh, 1)
    ig_t = jnp.where(lane_ok, ig_t, 0.0)
    fr_t = pltpu.roll(gt, fg_sh, 1)
    lf_t = jnp.where(lane_ok, -_softplus(-fr_t), 0.0)
    yield
    wv_t = jnp.dot(keep.astype(F32), lf_t, precision=HIGHEST, preferred_element_type=F32)
    ftot = jnp.sum(lf_t, axis=0, keepdims=True)
    yield
    wend_t = ftot - wv_t + ig_t
    mloc = jnp.max(wend_t, axis=0, keepdims=True)
    e_t = jnp.exp(wend_t - mloc)
    m_new = jnp.maximum(ftot + m_prev, mloc)
    a_prev = jnp.exp(ftot + m_prev - m_new)
    a_loc = jnp.exp(mloc - m_new)
    run = ig_t - wv_t
    sh = 1
    while sh < q:
        if reverse:
            moved, ok = pltpu.roll(run, q - sh, 0), r_i < q - sh
        else:
            moved, ok = pltpu.roll(run, sh, 0), r_i >= sh
        run = jnp.maximum(run, jnp.where(ok, moved, -jnp.inf))
        sh *= 2
    yield
    gcol_t = wv_t + m_prev
    mcomb_t = jnp.maximum(gcol_t, wv_t + run)
    wv = wv_t.T
    return dict(wm_t=wv_t - mcomb_t, rowterm=ig_t.T - wv, inter_t=jnp.exp(gcol_t - mcomb_t),
                floor_t=jnp.exp(-mcomb_t), e_r=e_t.T, a_prev=a_prev, a_loc=a_loc, m_new=m_new)


def _mlstm_head(p, qk_ref, v_ref, c_ref, h):
    q = CHUNK
    nh = MLSTM_HEADS
    dk = qk_ref.shape[-1] // (2 * nh)
    dv = v_ref.shape[-1] // nh
    qb = qk_ref[:, h * dk:(h + 1) * dk]
    kt = (qk_ref[:, nh * dk + h * dk: nh * dk + (h + 1) * dk].astype(F32) * (dk ** -0.5)).T
    vaug = jnp.concatenate([v_ref[:, h * dv:(h + 1) * dv], jnp.ones((q, LANES), BF16)], axis=1)
    caug = c_ref[h]
    lanes = lambda col: jnp.broadcast_to(col, (q, LANES))
    pm = jnp.exp(jnp.where(p["keep"], p["wm_t"][:, h:h + 1] + p["rowterm"][h:h + 1, :], -jnp.inf))
    inter = lanes(p["inter_t"][:, h:h + 1])
    floor = lanes(p["floor_t"][:, h:h + 1])
    yield
    s = (_dot(qb, kt.astype(BF16)) * pm).astype(BF16)
    qc = _dot(qb, caug.astype(BF16))
    yield
    sv = _dot(s, vaug)
    ap = p["a_prev"][:, h:h + 1]
    al = p["a_loc"][:, h:h + 1]
    c_new = ap * caug + al * _dot((kt * p["e_r"][h:h + 1, :]).astype(BF16), vaug)
    yield
    blocks = [sv[:, c:c + LANES] + inter * qc[:, c:c + LANES] for c in range(0, dv + LANES, LANES)]
    rden = 1.0 / jnp.maximum(jnp.abs(blocks[-1]), floor)
    h_out = jnp.concatenate([b * rden for b in blocks[:-1]], axis=1)
    return h_out, c_new


def _mlstm_kernel(qkf_ref, vf_ref, gtf_ref, gtf_next_ref, qkb_ref, vb_ref, gtb_ref, gtb_next_ref,
                  gb_ref, hf_ref, hb_ref, cf_ref, tabf_ref, rowf_ref, cb_ref, tabb_ref, rowb_ref):
    def park(tab_ref, row_ref, tables):
        for k, name in enumerate(MLSTM_TABLES):
            tab_ref[k] = tables[name]
        for k, name in enumerate(MLSTM_ROWS):
            row_ref[k] = tables[name]

    @pl.when(pl.program_id(1) == 0)
    def _():
        cf_ref[...] = jnp.zeros_like(cf_ref)
        cb_ref[...] = jnp.zeros_like(cb_ref)
        zero = jnp.zeros((1, LANES), F32)
        first = _lockstep([_mlstm_prep(_sub_rows(gtf_ref, 0, False), gb_ref, zero, reverse=False),
                           _mlstm_prep(_sub_rows(gtb_ref, 0, True), gb_ref, zero, reverse=True)])
        park(tabf_ref, rowf_ref, first[0])
        park(tabb_ref, rowb_ref, first[1])

    for s in range(SCAN_SUBSTEPS):
        last = s == SCAN_SUBSTEPS - 1
        _mlstm_substep(
            _sub_rows(qkf_ref, s, False), _sub_rows(vf_ref, s, False),
            _sub_rows(gtf_next_ref, 0, False) if last else _sub_rows(gtf_ref, s + 1, False),
            _sub_rows(qkb_ref, s, True), _sub_rows(vb_ref, s, True),
            _sub_rows(gtb_next_ref, 0, True) if last else _sub_rows(gtb_ref, s + 1, True),
            gb_ref, _sub_rows(hf_ref, s, False), _sub_rows(hb_ref, s, True),
            cf_ref, tabf_ref, rowf_ref, cb_ref, tabb_ref, rowb_ref, park)


def _mlstm_substep(qkf_ref, vf_ref, gtf_next_ref, qkb_ref, vb_ref, gtb_next_ref, gb_ref, hf_ref, hb_ref,
                   cf_ref, tabf_ref, rowf_ref, cb_ref, tabb_ref, rowb_ref, park):
    dv = vf_ref.shape[-1] // MLSTM_HEADS
    chains = []
    for (tab_ref, row_ref, qk_ref, v_ref, c_ref, h_ref, rev) in (
            (tabf_ref, rowf_ref, qkf_ref, vf_ref, cf_ref, hf_ref, False),
            (tabb_ref, rowb_ref, qkb_ref, vb_ref, cb_ref, hb_ref, True)):
        p = {name: tab_ref[k] for k, name in enumerate(MLSTM_TABLES)}
        p.update({name: row_ref[k] for k, name in enumerate(MLSTM_ROWS)})
        p["keep"] = _mlstm_keep(rev)
        chains.append((p, qk_ref, v_ref, c_ref, h_ref))
    units = [(p, qk_ref, v_ref, c_ref, h_ref, h)
             for (p, qk_ref, v_ref, c_ref, h_ref) in chains for h in range(MLSTM_HEADS)]
    results = _lockstep(
        [_mlstm_head(p, qk_ref, v_ref, c_ref, h) for (p, qk_ref, v_ref, c_ref, _, h) in units]
        + [_mlstm_prep(gtf_next_ref, gb_ref, chains[0][0]["m_new"], reverse=False),
           _mlstm_prep(gtb_next_ref, gb_ref, chains[1][0]["m_new"], reverse=True)])
    for (_, _, _, c_ref, h_ref, h), (h_out, c_new) in zip(units, results):
        h_ref[:, h * dv:(h + 1) * dv] = h_out.astype(h_ref.dtype)
        c_ref[h] = c_new
    park(tabf_ref, rowf_ref, results[-2])
    park(tabb_ref, rowb_ref, results[-1])


def _mlstm_scan(lay, qk, v, gates, gate_b):
    qkw, vw = qk.shape[1], v.shape[1]
    nh = MLSTM_HEADS
    dk, dv = qkw // (2 * nh), vw // nh
    fwd, bwd = _chunk_block(lay, False), _chunk_block(lay, True)
    ncc, nlc = lay.ctx_chunks // SCAN_SUBSTEPS, lay.lat_chunks // SCAN_SUBSTEPS
    out_f = lambda b, i: b * nlc + jnp.maximum(i - ncc, 0)
    out_b = lambda b, i: b * nlc + nlc - 1 - jnp.maximum(i - ncc, 0)
    assert CHUNK == LANES
    n_steps = ncc + nlc
    after = lambda order: (lambda b, i: order(b, jnp.minimum(i + 1, n_steps - 1)))
    blk = lambda width, order: pl.BlockSpec((SCAN_SUBSTEPS * CHUNK, width),
                                            lambda b, i: (order(b, i), 0))
    ins = lambda order: [blk(qkw, order), blk(vw, order), blk(LANES, order), blk(LANES, after(order))]
    state = [pltpu.VMEM((nh, dk, dv + LANES), F32),
             pltpu.VMEM((len(MLSTM_TABLES), CHUNK, LANES), F32),
             pltpu.VMEM((len(MLSTM_ROWS), 1, LANES), F32)]
    return pl.pallas_call(
        _mlstm_kernel,
        grid=(lay.bsz, n_steps),
        in_specs=ins(fwd) + ins(bwd) + [pl.BlockSpec((1, LANES), lambda b, i: (0, 0))],
        out_specs=[blk(vw, out_f), blk(vw, out_b)],
        out_shape=[jax.ShapeDtypeStruct((lay.lat_rows, vw), BF16)] * 2,
        scratch_shapes=state + state,
        compiler_params=_params(2),
        name="mlstm_scan",
    )(qk, v, gates, gates, qk, v, gates, gates, gate_b)


def _mlstm_out_up_kernel(hf_ref, hb_ref, o_ref_in, x_ref, g_ref, nw_ref, w_ref,
                         n2_ref, sh_ref, sc_ref, wup_ref, x_out_ref, a_ref, gl_ref):
    nh = MLSTM_HEADS
    dv = hf_ref.shape[-1] // nh
    for r0 in range(0, hf_ref.shape[0], OUT_ROW_BLOCK):
        rows = slice(r0, r0 + OUT_ROW_BLOCK)
        parts = []
        for h in range(nh):
            hh = (hf_ref[rows, h * dv:(h + 1) * dv].astype(F32)
                  + hb_ref[rows, h * dv:(h + 1) * dv].astype(F32))
            parts.append(_rms_scale(hh))
        hn = jnp.concatenate(parts, axis=1)
        y = (hn * nw_ref[...] * jax.nn.sigmoid(o_ref_in[rows, :].astype(F32))).astype(BF16)
        x_out_ref[rows, :] = x_ref[rows, :] + g_ref[...] * _dot(y, w_ref[...])
    h2 = _rms_scale(x_out_ref[...]) * n2_ref[...]
    hb2 = (h2 * (1.0 + sc_ref[...]) + sh_ref[...]).astype(BF16)
    f = a_ref.shape[1]
    for out_ref, s0 in ((a_ref, 0), (gl_ref, f)):
        for (c0, cw) in _col_chunks(s0, f):
            out_ref[:, c0 - s0:c0 - s0 + cw] = _dot(hb2, wup_ref[:, c0:c0 + cw]).astype(out_ref.dtype)


def _mlstm_out_up(lay, hf, hb, o_gate, xall, mods, layer, norm_w, w_out, norm2_w, w_up):
    vw = hf.shape[1]
    d = xall.shape[1]
    f = w_up.shape[2] // 2
    tm = lay.tm
    lat = lambda c: _rows_spec(tm, c, lambda j: j)
    allt = lambda c: _rows_spec(tm, c, lambda j: j + lay.ctx_tiles)
    row_fn = lambda j: j // lay.lat_tiles_per_batch
    return pl.pallas_call(
        _mlstm_out_up_kernel, grid=(lay.lat_tiles,),
        in_specs=[lat(vw), lat(vw), allt(vw), allt(d),
                  _mod_spec(d, layer, 2, row_fn), _full(norm_w), _layer_spec(w_out, 0),
                  _full(norm2_w), _mod_spec(d, layer, 3, row_fn), _mod_spec(d, layer, 4, row_fn),
                  _layer_spec(w_up, layer)],
        out_specs=[lat(d), lat(f), lat(f)],
        out_shape=[jax.ShapeDtypeStruct((lay.lat_rows, d), F32),
                   jax.ShapeDtypeStruct((lay.lat_rows, f), F32),
                   jax.ShapeDtypeStruct((lay.lat_rows, f), BF16)],
        compiler_params=_params(), name="mlstm_out_ffn_up",
    )(hf, hb, o_gate, xall, mods, norm_w, w_out, norm2_w, mods, mods, w_up)


def _ffn_down_kernel(*refs, lay, skip, final_norm):
    if final_norm:
        a_ref, ap_ref, an_ref, gl_ref, x_ref, g_ref, cw_ref, cb_ref, w_ref, fw_ref, o_ref = refs
    else:
        a_ref, ap_ref, an_ref, gl_ref, x_ref, g_ref, cw_ref, cb_ref, w_ref, o_ref = refs
    tm, f = a_ref.shape
    gw = GRID_WIDTH
    j = pl.program_id(0) + skip
    is_ctx = j < lay.ctx_tiles
    pos = lay.batch_pos(j)
    has_prev = jnp.logical_and(jnp.logical_not(is_ctx), pos > 0)
    has_next = jnp.logical_and(jnp.logical_not(is_ctx), pos < lay.lat_tiles_per_batch - 1)

    cw = FFN_COL_CHUNK
    vert = jnp.where(is_ctx, 0.0, 1.0)
    sb = FFN_SHIFT_BLOCK
    ri = lax.broadcasted_iota(jnp.int32, (sb, sb), 0)
    ci = lax.broadcasted_iota(jnp.int32, (sb, sb), 1)
    colpos = jnp.where(is_ctx, ri % lay.n_ctx, ri & (gw - 1))
    last_col = jnp.where(is_ctx, lay.n_ctx - 1, gw - 1)
    take_left = jnp.where(colpos != 0, jnp.where(ci == ri - 1, 1.0, 0.0), 0.0)
    take_right = jnp.where(colpos != last_col, jnp.where(ci == ri + 1, 1.0, 0.0), 0.0)
    shift = jnp.concatenate([take_left, take_right], axis=1).astype(BF16)

    def taps(c0):
        cs = slice(c0, c0 + cw)
        up = jnp.where(has_prev, ap_ref[:, cs], 0.0)
        dn = jnp.where(has_next, an_ref[:, cs], 0.0)
        slabs = [jnp.concatenate([up, a_ref[0:tm - gw, cs]], axis=0), a_ref[:, cs],
                 jnp.concatenate([a_ref[gw:tm, cs], dn], axis=0)]

        def vsum(dj, parts, dtype):
            k = lambda di: 3 * (di + 1) + (dj + 1)
            return (parts[0] * (cw_ref[k(-1):k(-1) + 1, cs] * vert).astype(dtype)
                    + parts[1] * cw_ref[k(0):k(0) + 1, cs].astype(dtype)
                    + parts[2] * (cw_ref[k(1):k(1) + 1, cs] * vert).astype(dtype))

        slabs_b = [s.astype(BF16) for s in slabs]
        left, right = vsum(-1, slabs_b, BF16), vsum(1, slabs_b, BF16)
        sides = jnp.concatenate(
            [_dot(shift, jnp.concatenate([left[r0:r0 + sb], right[r0:r0 + sb]], axis=0))
             for r0 in range(0, tm, sb)], axis=0)
        return vsum(0, slabs, F32) + cb_ref[:, cs], sides

    def finish(c0, centre, sides, acc_out):
        cs = slice(c0, c0 + cw)
        mid = (_silu(centre + sides) * gl_ref[:, cs].astype(F32)).astype(BF16)
        part = _dot(mid, w_ref[cs, :])
        return part if acc_out is None else acc_out + part

    acc_out = None
    pending = None
    for c0 in range(0, f, cw):
        cur = (c0,) + taps(c0)
        if pending is not None:
            acc_out = finish(*pending, acc_out)
        pending = cur
    acc_out = finish(*pending, acc_out)
    out = x_ref[...] + g_ref[...] * acc_out
    if final_norm:
        out = _rms_scale(out) * fw_ref[...]
    o_ref[...] = out


def _ffn_down(lay, a, gl, xres, mods, layer, conv_w9, conv_b, w_down, *, skip=0, final_w=None):
    tt, f = a.shape
    d = xres.shape[1]
    tm = lay.tm
    gw = GRID_WIDTH
    assert f % FFN_COL_CHUNK == 0 and tm % FFN_SHIFT_BLOCK == 0
    assert FFN_SHIFT_BLOCK % gw == 0 and FFN_SHIFT_BLOCK % lay.n_ctx == 0
    n_tiles = tt // tm
    hb = tm // gw
    tok = lambda c: _rows_spec(tm, c, lambda j: j)
    in_specs = [
        tok(f),
        _rows_spec(gw, f, lambda j: jnp.maximum(j * hb - 1, 0)),
        _rows_spec(gw, f, lambda j: jnp.minimum((j + 1) * hb, tt // gw - 1)),
        tok(f), tok(d), _mod_spec(d, layer, 5, lambda j: lay.mod_row(j + skip)),
        _full(conv_w9), _full(conv_b), _layer_spec(w_down, layer)]
    args = [a, a, a, gl, xres, mods, conv_w9, conv_b, w_down]
    if final_w is not None:
        in_specs.append(_full(final_w))
        args.append(final_w)
    kern = functools.partial(_ffn_down_kernel, lay=lay, skip=skip, final_norm=final_w is not None)
    return pl.pallas_call(
        kern, grid=(n_tiles,), in_specs=in_specs, out_specs=tok(d),
        out_shape=jax.ShapeDtypeStruct((tt, d), F32),
        compiler_params=_params(), name=f"ffn_down_l{layer}",
    )(*args)


def _pad_cols(a, n):
    return jnp.pad(a, [(0, 0)] * (a.ndim - 1) + [(0, n - a.shape[-1])])


def kernel(x, c, ctx, c_ctx, ada_w, ada_b, norm1_w, norm2_w, ssm_w_in, ssm_conv_w, ssm_conv_b, ssm_dt_bias, ssm_a_log, ssm_d, ssm_norm_w, ssm_w_out, mlstm_w_in, mlstm_conv_w, mlstm_conv_b, mlstm_gate_b, mlstm_norm_w, mlstm_w_out, ffn_w_up, ffn_conv_w, ffn_conv_b, ffn_w_down, final_norm_w):
    bsz, seq, d = x.shape
    n_ctx = ctx.shape[1]
    depth = ada_w.shape[0]
    assert depth == 2 and ssm_w_in.shape[0] == 1 and mlstm_w_in.shape[0] == 1
    lay = _Layout(bsz, n_ctx, seq)
    x_flat = x.reshape(bsz * seq, d)
    ctx_flat = ctx.reshape(bsz * n_ctx, d)

    n_rows = -(-(bsz + 1) // SUBLANES) * SUBLANES
    cvec = jnp.zeros((n_rows, d), F32).at[:bsz].set(c).at[bsz].set(c_ctx)
    mods = _ada(cvec, ada_w, ada_b).reshape(depth, n_rows, 6, 1, d)

    w_up = ffn_w_up.astype(BF16)
    w_down = ffn_w_down.astype(BF16)
    f = ffn_conv_b.shape[1]
    conv9 = ffn_conv_w.reshape(depth, 9, f)

    heads = ssm_d.shape[1]
    assert 2 * heads <= LANES
    inner = heads * SSM_HEAD_DIM
    bc_w = 2 * SSM_GROUPS * SSM_STATE
    conv_ch = inner + bc_w
    w_in = _pad_cols(ssm_w_in, 2 * inner + bc_w + LANES).astype(BF16)
    splits = [(0, inner), (inner, conv_ch), (inner + conv_ch, LANES)]
    z, xbc, dt = _project(
        lay, (ctx_flat, x_flat), norm1_w[0:1], mods, 0, 0, 1, w_in, splits, (BF16, BF16, F32),
        conv=(1, ssm_conv_w[0], ssm_conv_b[0:1]), name="in_proj_l0")
    dtb = _pad_cols(ssm_dt_bias[0].reshape(1, -1), LANES)
    alog = _pad_cols(ssm_a_log[0].reshape(1, -1), LANES)
    y_f, y_b = _ssd_scan(lay, xbc, dt, dtb, alog, ssm_d[0], heads=heads)
    xall = _ssm_out(lay, y_f, y_b, z, (ctx_flat, x_flat), mods, 0, ssm_norm_w[0:1],
                    ssm_w_out.astype(BF16))

    a, gl = _project(lay, (xall,), norm2_w[0:1], mods, 0, 3, 4, w_up, [(0, f), (f, f)], (F32, BF16),
                     name="ffn_up_l0")
    xall = _ffn_down(lay, a, gl, xall, mods, 0, conv9[0], ffn_conv_b[0:1], w_down)

    qkw = mlstm_conv_b.shape[1]
    vw = mlstm_norm_w.shape[1]
    w_in = _pad_cols(mlstm_w_in, qkw + 2 * vw + LANES).astype(BF16)
    splits = [(0, qkw), (qkw, vw), (qkw + vw, vw), (qkw + 2 * vw, LANES)]
    qk, v, o_gate, gates = _project(
        lay, (xall,), norm1_w[1:2], mods, 1, 0, 1, w_in, splits, (BF16, BF16, BF16, F32),
        conv=(0, mlstm_conv_w[0], mlstm_conv_b[0:1]), name="in_proj_l1")
    gate_b = _pad_cols(mlstm_gate_b[0:1], LANES)
    h_f, h_b = _mlstm_scan(lay, qk, v, gates, gate_b)
    xlat, a, gl = _mlstm_out_up(lay, h_f, h_b, o_gate, xall, mods, 1, mlstm_norm_w[0:1],
                                mlstm_w_out.astype(BF16), norm2_w[1:2], w_up)
    out = _ffn_down(lay, a, gl, xlat, mods, 1, conv9[1], ffn_conv_b[1:2], w_down,
                    skip=lay.ctx_tiles, final_w=final_norm_w.reshape(1, d))
    return out.reshape(bsz, seq, d)
```

```python
import functools

import jax
import jax.numpy as jnp
from jax import lax
from jax.experimental import pallas as pl
from jax.experimental.pallas import tpu as pltpu

F32 = jnp.float32
BF16 = jnp.bfloat16
HIGHEST = lax.Precision.HIGHEST

NORM_EPS = 1e-6
GRID_WIDTH = 64
SSM_HEAD_DIM = 64
SSM_GROUPS = 4
SSM_STATE = 128
MLSTM_HEADS = 4
GATE_CAP = 15.0
CHUNK = 128

LANES = 128
SUBLANES = 8
TOKEN_TILE = 512
COL_CHUNK = 512
FFN_COL_CHUNK = 256
FFN_SHIFT_BLOCK = 256
SCAN_SUBSTEPS = 2
OUT_ROW_BLOCK = 256
OUT_COL_CHUNK = 256
VMEM_LIMIT = 56 * 1024 * 1024
VMEM_LIMIT_FUSED = 61 * 1024 * 1024


class _Layout:
    def __init__(self, bsz, n_ctx, seq, tm=TOKEN_TILE):
        assert tm % n_ctx == 0 and (bsz * n_ctx) % tm == 0 and seq % tm == 0
        assert n_ctx % CHUNK == 0 and seq % CHUNK == 0 and seq % GRID_WIDTH == 0
        self.bsz, self.n_ctx, self.seq, self.tm = bsz, n_ctx, seq, tm
        self.ctx_rows, self.lat_rows = bsz * n_ctx, bsz * seq
        self.rows = self.ctx_rows + self.lat_rows
        self.ctx_tiles = self.ctx_rows // tm
        self.lat_tiles_per_batch = seq // tm
        self.lat_tiles = bsz * self.lat_tiles_per_batch
        self.tiles = self.ctx_tiles + self.lat_tiles
        self.ctx_chunks, self.lat_chunks = n_ctx // CHUNK, seq // CHUNK

    def mod_row(self, j):
        return jnp.where(j < self.ctx_tiles, self.bsz, (j - self.ctx_tiles) // self.lat_tiles_per_batch)

    def batch_pos(self, j):
        return (j - self.ctx_tiles) % self.lat_tiles_per_batch


def _params(n_axes=1, vmem_limit=VMEM_LIMIT):
    return pltpu.CompilerParams(
        dimension_semantics=("arbitrary",) * n_axes, vmem_limit_bytes=vmem_limit)


def _col_chunks(start, width, step=COL_CHUNK):
    out, c = [], start
    while c < start + width:
        w = min(step, start + width - c)
        out.append((c, w))
        c += w
    return out


def _silu(v):
    half = 0.5 * v
    return half + half * jnp.tanh(half)


def _softplus(v):
    return jnp.maximum(v, 0.0) + jnp.log1p(jnp.exp(-jnp.abs(v)))


def _rms_scale(v):
    return v * lax.rsqrt(jnp.mean(v * v, axis=-1, keepdims=True) + NORM_EPS)


def _dot(a, b):
    return jnp.dot(a, b, preferred_element_type=F32)


def _dot_nt(a, b):
    return lax.dot_general(a, b, (((1,), (1,)), ((), ())), preferred_element_type=F32)


def _lockstep_inner(gens):
    results = [None] * len(gens)
    live = list(range(len(gens)))
    while live:
        for i in list(live):
            try:
                next(gens[i])
            except StopIteration as done:
                results[i] = done.value
                live.remove(i)
        yield
    return results


def _lockstep(gens):
    it = _lockstep_inner(gens)
    while True:
        try:
            next(it)
        except StopIteration as done:
            return done.value


def _full(a):
    return pl.BlockSpec(a.shape, lambda *_: (0,) * a.ndim)


def _layer_spec(a, layer):
    return pl.BlockSpec((None,) + a.shape[1:], lambda *_: (layer,) + (0,) * (a.ndim - 1),
                        pipeline_mode=pl.Buffered(1))


def _mod_spec(d, layer, k, row_fn):
    return pl.BlockSpec((None, None, None, 1, d), lambda j: (layer, row_fn(j), k, 0, 0))


def _rows_spec(rows, width, idx_fn):
    return pl.BlockSpec((rows, width), lambda j: (idx_fn(j), 0))


def _ada_kernel(c_ref, w_ref, b_ref, o_ref):
    s = _silu(c_ref[...])
    o_ref[...] = jnp.dot(s, w_ref[...], precision=HIGHEST, preferred_element_type=F32) + b_ref[...]


def _ada(cvec, ada_w, ada_b):
    depth, d, n = ada_w.shape
    rows = cvec.shape[0]
    tn = 512
    return pl.pallas_call(
        _ada_kernel,
        grid=(depth, n // tn),
        in_specs=[
            pl.BlockSpec((rows, d), lambda l, j: (0, 0)),
            pl.BlockSpec((None, d, tn), lambda l, j: (l, 0, j)),
            pl.BlockSpec((None, 1, tn), lambda l, j: (l, 0, j)),
        ],
        out_specs=pl.BlockSpec((None, rows, tn), lambda l, j: (l, 0, j)),
        out_shape=jax.ShapeDtypeStruct((depth, rows, n), F32),
        compiler_params=_params(2),
        name="ada_mod",
    )(cvec, ada_w, ada_b.reshape(depth, 1, n))


def _tok_specs(lay, arrs, rows, shift=0):
    per = lay.tm // rows
    if len(arrs) == 1:
        (a,) = arrs
        hi = a.shape[0] // rows - 1
        return [_rows_spec(rows, a.shape[1], lambda j: jnp.clip(j * per + shift, 0, hi))]
    ctx, lat = arrs
    hi_c = ctx.shape[0] // rows - 1
    hi_l = lat.shape[0] // rows - 1
    return [_rows_spec(rows, ctx.shape[1], lambda j: jnp.clip(j * per + shift, 0, hi_c)),
            _rows_spec(rows, lat.shape[1],
                       lambda j: jnp.clip((j - lay.ctx_tiles) * per + shift, 0, hi_l))]


def _tok_read(refs, is_ctx):
    if len(refs) == 1:
        return refs[0][...]
    return jnp.where(is_ctx, refs[0][...], refs[1][...])


def _proj_kernel(*refs, lay, n_src, splits, conv_idx, skip):
    refs = list(refs)
    x_refs = [refs.pop(0) for _ in range(n_src)]
    if conv_idx is not None:
        xp_refs = [refs.pop(0) for _ in range(n_src)]
        xn_refs = [refs.pop(0) for _ in range(n_src)]
    nw_ref, sh_ref, sc_ref, w_ref = refs[:4]
    refs = refs[4:]
    if conv_idx is not None:
        cw_ref, cb_ref = refs[:2]
        refs = refs[2:]
    outs = refs
    j = pl.program_id(0) + skip
    is_ctx = j < lay.ctx_tiles
    tm = lay.tm

    def prep(v):
        h = _rms_scale(v) * nw_ref[...]
        return h * (1.0 + sc_ref[...]) + sh_ref[...]

    h_main = prep(_tok_read(x_refs, is_ctx))
    hb = h_main.astype(BF16)
    if conv_idx is not None:
        pos = lay.batch_pos(j)
        is_first = jnp.logical_or(is_ctx, pos == 0)
        is_last = jnp.logical_or(is_ctx, pos == lay.lat_tiles_per_batch - 1)
        hp = jnp.where(is_first, 0.0, prep(_tok_read(xp_refs, is_ctx)))
        hn = jnp.where(is_last, 0.0, prep(_tok_read(xn_refs, is_ctx)))
        hext = jnp.concatenate([hp, h_main, hn], axis=0).astype(BF16)
        te = tm + 2 * SUBLANES
        seam_rows = lax.broadcasted_iota(jnp.int32, (te, 1), 0) - SUBLANES
        keep_up = jnp.where(jnp.logical_and(is_ctx, seam_rows % lay.n_ctx == 0), 0.0, 1.0)
        keep_dn = jnp.where(jnp.logical_and(is_ctx, seam_rows % lay.n_ctx == lay.n_ctx - 1), 0.0, 1.0)
    for oi, (s0, sw) in enumerate(splits):
        for (c0, cw) in _col_chunks(s0, sw):
            if oi == conv_idx:
                r = _dot(hext, w_ref[:, c0:c0 + cw])
                k0 = c0 - s0

                def conv(up, dn):
                    y = (up * cw_ref[0:1, k0:k0 + cw] + r * cw_ref[1:2, k0:k0 + cw]
                         + dn * cw_ref[2:3, k0:k0 + cw] + cb_ref[:, k0:k0 + cw])
                    return _silu(y[SUBLANES:SUBLANES + tm, :])

                up, dn = pltpu.roll(r, 1, 0), pltpu.roll(r, te - 1, 0)
                r = conv(up * keep_up, dn * keep_dn)
            else:
                r = _dot(hb, w_ref[:, c0:c0 + cw])
            outs[oi][:, c0 - s0:c0 - s0 + cw] = r.astype(outs[oi].dtype)


def _project(lay, srcs, norm_w, mods, layer, k_shift, k_scale, w, splits, out_dtypes, *,
             conv=None, skip=0, name):
    d = srcs[0].shape[1]
    tm = lay.tm
    n_tiles = lay.tiles - skip
    row_fn = lambda j: lay.mod_row(j + skip)
    in_specs = _tok_specs(lay, srcs, tm)
    args = list(srcs)
    conv_idx = None
    if conv is not None:
        assert skip == 0
        conv_idx, conv_w, conv_b = conv
        in_specs += _tok_specs(lay, srcs, SUBLANES, shift=-1)
        in_specs += _tok_specs(lay, srcs, SUBLANES, shift=tm // SUBLANES)
        args += list(srcs) * 2
    in_specs += [_full(norm_w), _mod_spec(d, layer, k_shift, row_fn),
                 _mod_spec(d, layer, k_scale, row_fn), _layer_spec(w, layer if w.shape[0] > 1 else 0)]
    args += [norm_w, mods, mods, w]
    if conv is not None:
        in_specs += [_full(conv_w), _full(conv_b)]
        args += [conv_w, conv_b]
    out_specs = [_rows_spec(tm, sw, lambda j: j) for (_, sw) in splits]
    out_shape = [jax.ShapeDtypeStruct((n_tiles * tm, sw), dt) for (_, sw), dt in zip(splits, out_dtypes)]
    kern = functools.partial(_proj_kernel, lay=lay, n_src=len(srcs), splits=tuple(splits),
                             conv_idx=conv_idx, skip=skip)
    return pl.pallas_call(
        kern, grid=(n_tiles,), in_specs=in_specs, out_specs=out_specs, out_shape=out_shape,
        compiler_params=_params(), name=name,
    )(*args)


SSD_TABLES = ("w", "w_t", "dts", "dte", "ew", "etot")


def _ssd_prep(dt_ref, dtb_ref, alog_ref, *, reverse):
    q = CHUNK
    r_i = lax.broadcasted_iota(jnp.int32, (q, q), 0)
    c_i = lax.broadcasted_iota(jnp.int32, (q, q), 1)
    tri = (c_i >= r_i) if reverse else (c_i <= r_i)
    dts_t = _softplus(dt_ref[...] + dtb_ref[...])
    la_t = dts_t * (-jnp.exp(alog_ref[...]))
    yield
    w_t = jnp.dot(tri.astype(F32), la_t, precision=HIGHEST, preferred_element_type=F32)
    yield
    w = w_t.T
    dts = dts_t.T
    yield
    tot = w[:, 0:1] if reverse else w[:, q - 1:q]
    dte = jnp.exp(tot - w)
    ew = jnp.exp(w)
    etot = jnp.broadcast_to(jnp.exp(tot), (LANES, SSM_STATE))
    return dict(w=w, w_t=w_t, dts=dts, dte=dte, ew=ew, etot=etot)


def _ssd_mask(reverse):
    r_i = lax.broadcasted_iota(jnp.int32, (CHUNK, CHUNK), 0)
    c_i = lax.broadcasted_iota(jnp.int32, (CHUNK, CHUNK), 1)
    return (r_i >= c_i) if reverse else (r_i <= c_i)


def _ssd_group(p, xbc_ref, st_ref, g, *, reverse, heads, skip_ref=None):
    hd, ng, ns = SSM_HEAD_DIM, SSM_GROUPS, SSM_STATE
    hpg = heads // ng
    inner = heads * hd
    dir_off = heads if reverse else 0
    w, w_t = p["w"], p["w_t"]
    bb = xbc_ref[:, inner + g * ns: inner + (g + 1) * ns]
    cb = xbc_ref[:, inner + ng * ns + g * ns: inner + ng * ns + (g + 1) * ns]
    cbt = _dot_nt(bb, cb)
    st_g = st_ref[g]
    yint = _dot_nt(st_g.astype(BF16), cb)
    y_parts, xdec_parts = [], []
    for pr in range(hpg // 2):
        col = (g * hpg + 2 * pr) * hd
        blk_t = xbc_ref[:, col:col + 2 * hd].astype(F32).T
        for half in range(2):
            r = 2 * pr + half
            hh = dir_off + g * hpg + r
            xd = blk_t[half * hd:(half + 1) * hd, :] * p["dts"][hh:hh + 1, :]
            expo = jnp.where(p["mask_sl"], w[hh:hh + 1, :] - w_t[:, hh:hh + 1], -jnp.inf)
            mt = (cbt * jnp.exp(expo)).astype(BF16)
            y_h = (_dot(xd.astype(BF16), mt)
                   + yint[r * hd:(r + 1) * hd, :] * p["ew"][hh:hh + 1, :])
            if skip_ref is not None:
                y_h = y_h + skip_ref[g * hpg + r] * blk_t[half * hd:(half + 1) * hd, :]
            y_parts.append(y_h)
            xdec_parts.append((xd * p["dte"][hh:hh + 1, :]).astype(BF16))
    xdec = jnp.concatenate(xdec_parts, axis=0)
    sloc = _dot(xdec, bb)
    new_state = [st_g[r * hd:(r + 1) * hd, :] * p["etot"][dir_off + g * hpg + r:dir_off + g * hpg + r + 1, :]
                 + sloc[r * hd:(r + 1) * hd, :] for r in range(hpg)]
    y_blocks = [jnp.concatenate(y_parts[2 * pr:2 * pr + 2], axis=0).T for pr in range(hpg // 2)]
    return y_blocks, new_state


def _sub_rows(ref, s, reverse):
    idx = (SCAN_SUBSTEPS - 1 - s) if reverse else s
    return ref.at[pl.ds(idx * CHUNK, CHUNK), :]


def _ssd_kernel(xf_ref, dtf_ref, dtf_next_ref, xb_ref, dtb2_ref, dtb_next_ref, dtb_ref, alog_ref,
                d_ref, yf_ref, yb_ref, stf_ref, stb_ref, tabf_ref, tabb_ref, *, heads):
    def park(tab_ref, tables):
        for k, name in enumerate(SSD_TABLES):
            tab_ref[k] = tables[name]

    @pl.when(pl.program_id(1) == 0)
    def _():
        stf_ref[...] = jnp.zeros_like(stf_ref)
        stb_ref[...] = jnp.zeros_like(stb_ref)
        first = _lockstep([_ssd_prep(_sub_rows(dtf_ref, 0, False), dtb_ref, alog_ref, reverse=False),
                           _ssd_prep(_sub_rows(dtb2_ref, 0, True), dtb_ref, alog_ref, reverse=True)])
        park(tabf_ref, first[0])
        park(tabb_ref, first[1])

    for s in range(SCAN_SUBSTEPS):
        last = s == SCAN_SUBSTEPS - 1
        _ssd_substep(
            _sub_rows(xf_ref, s, False),
            _sub_rows(dtf_next_ref, 0, False) if last else _sub_rows(dtf_ref, s + 1, False),
            _sub_rows(xb_ref, s, True),
            _sub_rows(dtb_next_ref, 0, True) if last else _sub_rows(dtb2_ref, s + 1, True),
            dtb_ref, alog_ref, d_ref, _sub_rows(yf_ref, s, False), _sub_rows(yb_ref, s, True),
            stf_ref, stb_ref, tabf_ref, tabb_ref, park, heads=heads)


def _ssd_substep(xf_ref, dtf_next_ref, xb_ref, dtb_next_ref, dtb_ref, alog_ref, d_ref, yf_ref, yb_ref,
                 stf_ref, stb_ref, tabf_ref, tabb_ref, park, *, heads):
    hd = SSM_HEAD_DIM
    hpg = heads // SSM_GROUPS
    chains = []
    for (tab_ref, x_ref, st_ref, y_ref, rev) in ((tabf_ref, xf_ref, stf_ref, yf_ref, False),
                                                 (tabb_ref, xb_ref, stb_ref, yb_ref, True)):
        p = {name: tab_ref.at[k] for k, name in enumerate(SSD_TABLES)}
        p["mask_sl"] = _ssd_mask(rev)
        chains.append((p, x_ref, st_ref, y_ref, rev))
    upcoming = _lockstep_inner([_ssd_prep(dtf_next_ref, dtb_ref, alog_ref, reverse=False),
                                _ssd_prep(dtb_next_ref, dtb_ref, alog_ref, reverse=True)])
    nxt = None
    for g in range(SSM_GROUPS):
        results = [_ssd_group(p, x_ref, st_ref, g, reverse=rev, heads=heads,
                              skip_ref=None if rev else d_ref)
                   for (p, x_ref, st_ref, _, rev) in chains]
        if nxt is None:
            try:
                next(upcoming)
            except StopIteration as done:
                nxt = done.value
        for (_, _, st_ref, y_ref, _), (y_blocks, new_state) in zip(chains, results):
            for r, s_new in enumerate(new_state):
                st_ref[g, r * hd:(r + 1) * hd, :] = s_new
            for pr, blk in enumerate(y_blocks):
                col = (g * hpg + 2 * pr) * hd
                y_ref[:, col:col + 2 * hd] = blk.astype(y_ref.dtype)
    while nxt is None:
        try:
            next(upcoming)
        except StopIteration as done:
            nxt = done.value
    park(tabf_ref, nxt[0])
    park(tabb_ref, nxt[1])


def _chunk_block(lay, reverse):
    assert lay.ctx_chunks % SCAN_SUBSTEPS == 0 and lay.lat_chunks % SCAN_SUBSTEPS == 0
    ncc, nlc = lay.ctx_chunks // SCAN_SUBSTEPS, lay.lat_chunks // SCAN_SUBSTEPS

    def block(b, i):
        in_ctx = i < ncc
        c_ctx = (ncc - 1 - i) if reverse else i
        c_lat = (nlc - 1 - (i - ncc)) if reverse else (i - ncc)
        return jnp.where(in_ctx, b * ncc + c_ctx, lay.bsz * ncc + b * nlc + c_lat)

    return block


def _ssd_scan(lay, xbc, dt, dtb, alog, d_skip, *, heads):
    cch = xbc.shape[1]
    inner = heads * SSM_HEAD_DIM
    assert CHUNK == LANES == SSM_STATE
    fwd, bwd = _chunk_block(lay, False), _chunk_block(lay, True)
    n_steps = (lay.ctx_chunks + lay.lat_chunks) // SCAN_SUBSTEPS
    after = lambda order: (lambda b, i: order(b, jnp.minimum(i + 1, n_steps - 1)))
    blk = lambda width, order: pl.BlockSpec((SCAN_SUBSTEPS * CHUNK, width),
                                            lambda b, i: (order(b, i), 0))
    state = pltpu.VMEM((SSM_GROUPS, inner // SSM_GROUPS, SSM_STATE), F32)
    tables = pltpu.VMEM((len(SSD_TABLES), LANES, CHUNK), F32)
    return pl.pallas_call(
        functools.partial(_ssd_kernel, heads=heads),
        grid=(lay.bsz, n_steps),
        in_specs=[blk(cch, fwd), blk(LANES, fwd), blk(LANES, after(fwd)),
                  blk(cch, bwd), blk(LANES, bwd), blk(LANES, after(bwd)),
                  pl.BlockSpec((1, LANES), lambda b, i: (0, 0)),
                  pl.BlockSpec((1, LANES), lambda b, i: (0, 0)),
                  pl.BlockSpec(memory_space=pltpu.SMEM)],
        out_specs=[blk(inner, fwd), blk(inner, bwd)],
        out_shape=[jax.ShapeDtypeStruct((lay.rows, inner), BF16)] * 2,
        scratch_shapes=[state, state, tables, tables],
        compiler_params=_params(2),
        name="ssd_scan",
    )(xbc, dt, dt, xbc, dt, dt, dtb, alog, d_skip)


def _up_project(x_new, n2_ref, sh_ref, sc_ref, wup_ref, a_ref, gl_ref):
    h2 = _rms_scale(x_new) * n2_ref[...]
    hb2 = (h2 * (1.0 + sc_ref[...]) + sh_ref[...]).astype(BF16)
    f = a_ref.shape[1]
    for out_ref, s0 in ((a_ref, 0), (gl_ref, f)):
        for (c0, cw) in _col_chunks(s0, f):
            out_ref[:, c0 - s0:c0 - s0 + cw] = _dot(hb2, wup_ref[:, c0:c0 + cw]).astype(out_ref.dtype)


def _ssm_out_kernel(*refs, lay):
    (yf_ref, yb_ref, z_ref, xc_ref, xl_ref, g_ref, nw_ref, w_ref,
     n2_ref, sh_ref, sc_ref, wup_ref, o_ref, a_ref, gl_ref) = refs
    is_ctx = pl.program_id(0) < lay.ctx_tiles
    inner = z_ref.shape[1]
    acc = None
    sumsq = None
    for (c0, cw) in _col_chunks(0, inner, OUT_COL_CHUNK):
        cs = slice(c0, c0 + cw)
        z = z_ref[:, cs].astype(F32)
        y = (yf_ref[:, cs].astype(F32) + yb_ref[:, cs].astype(F32)) * _silu(z)
        sq = y * y
        part_sq = functools.reduce(lambda u, v: u + v, [sq[:, k:k + LANES] for k in range(0, cw, LANES)])
        sumsq = part_sq if sumsq is None else sumsq + part_sq
        part = _dot((y * nw_ref[:, cs]).astype(BF16), w_ref[cs, :])
        acc = part if acc is None else acc + part
    scale = lax.rsqrt(jnp.sum(sumsq, axis=-1, keepdims=True) * (1.0 / inner) + NORM_EPS)
    resid = jnp.where(is_ctx, xc_ref[...], xl_ref[...])
    x_new = resid + g_ref[...] * (scale * acc)
    o_ref[...] = x_new
    _up_project(x_new, n2_ref, sh_ref, sc_ref, wup_ref, a_ref, gl_ref)


def _ssm_out_up(lay, yf, yb, z, srcs, mods, layer, norm_w, w_out, norm2_w, w_up):
    inner = yf.shape[1]
    d = srcs[0].shape[1]
    f = w_up.shape[2] // 2
    tm = lay.tm
    tok = lambda c: _rows_spec(tm, c, lambda j: j)
    return pl.pallas_call(
        functools.partial(_ssm_out_kernel, lay=lay), grid=(lay.tiles,),
        in_specs=[tok(inner), tok(inner), tok(inner)]
        + _tok_specs(lay, srcs, tm)
        + [_mod_spec(d, layer, 2, lay.mod_row), _full(norm_w), _layer_spec(w_out, 0),
           _full(norm2_w), _mod_spec(d, layer, 3, lay.mod_row), _mod_spec(d, layer, 4, lay.mod_row),
           _layer_spec(w_up, layer)],
        out_specs=[tok(d), tok(f), tok(f)],
        out_shape=[jax.ShapeDtypeStruct((lay.rows, d), F32),
                   jax.ShapeDtypeStruct((lay.rows, f), F32),
                   jax.ShapeDtypeStruct((lay.rows, f), BF16)],
        compiler_params=_params(vmem_limit=VMEM_LIMIT_FUSED), name="ssm_out_ffn_up",
    )(yf, yb, z, *srcs, mods, norm_w, w_out, norm2_w, mods, mods, w_up)


MLSTM_TABLES = ("wm_t", "rowterm", "inter_t", "floor_t", "e_r")
MLSTM_ROWS = ("a_prev", "a_loc", "m_new")


def _mlstm_keep(reverse):
    r_i = lax.broadcasted_iota(jnp.int32, (CHUNK, CHUNK), 0)
    c_i = lax.broadcasted_iota(jnp.int32, (CHUNK, CHUNK), 1)
    return (c_i >= r_i) if reverse else (c_i <= r_i)


def _mlstm_prep(gt_ref, gb_ref, m_prev, *, reverse):
    q = CHUNK
    nh = MLSTM_HEADS
    d = 1 if reverse else 0

    r_i = lax.broadcasted_iota(jnp.int32, (q, q), 0)
    c_i = lax.broadcasted_iota(jnp.int32, (q, q), 1)
    keep = _mlstm_keep(reverse)

    gt = GATE_CAP * jnp.tanh((gt_ref[...] + gb_ref[...]) / GATE_CAP)
    lane_ok = c_i < nh
    ig_sh = (LANES - nh * d) % LANES
    fg_sh = LANES - (2 * nh + nh * d)
    ig_t = gt if ig_sh == 0 else pltpu.roll(gt, ig_sh, 1)
    ig_t = jnp.where(lane_ok, ig_t, 0.0)
    fr_t = pltpu.roll(gt, fg_sh, 1)
    lf_t = jnp.where(lane_ok, -_softplus(-fr_t), 0.0)
    yield
    wv_t = jnp.dot(keep.astype(F32), lf_t, precision=HIGHEST, preferred_element_type=F32)
    ftot = jnp.sum(lf_t, axis=0, keepdims=True)
    yield
    wend_t = ftot - wv_t + ig_t
    mloc = jnp.max(wend_t, axis=0, keepdims=True)
    e_t = jnp.exp(wend_t - mloc)
    m_new = jnp.maximum(ftot + m_prev, mloc)
    a_prev = jnp.exp(ftot + m_prev - m_new)
    a_loc = jnp.exp(mloc - m_new)
    run = ig_t - wv_t
    sh = 1
    while sh < q:
        if reverse:
            moved, ok = pltpu.roll(run, q - sh, 0), r_i < q - sh
        else:
            moved, ok = pltpu.roll(run, sh, 0), r_i >= sh
        run = jnp.maximum(run, jnp.where(ok, moved, -jnp.inf))
        sh *= 2
    yield
    gcol_t = wv_t + m_prev
    mcomb_t = jnp.maximum(gcol_t, wv_t + run)
    wv = wv_t.T
    return dict(wm_t=wv_t - mcomb_t, rowterm=ig_t.T - wv, inter_t=jnp.exp(gcol_t - mcomb_t),
                floor_t=jnp.exp(-mcomb_t), e_r=e_t.T, a_prev=a_prev, a_loc=a_loc, m_new=m_new)


def _mlstm_head(p, qk_ref, v_ref, c_ref, h):
    q = CHUNK
    nh = MLSTM_HEADS
    dk = qk_ref.shape[-1] // (2 * nh)
    dv = v_ref.shape[-1] // nh
    qb = qk_ref[:, h * dk:(h + 1) * dk]
    kt = (qk_ref[:, nh * dk + h * dk: nh * dk + (h + 1) * dk].astype(F32) * (dk ** -0.5)).T
    vaug = jnp.concatenate([v_ref[:, h * dv:(h + 1) * dv], jnp.ones((q, LANES), BF16)], axis=1)
    caug = c_ref[h]
    lanes = lambda col: jnp.broadcast_to(col, (q, LANES))
    pm = jnp.exp(jnp.where(p["keep"], p["wm_t"][:, h:h + 1] + p["rowterm"][h:h + 1, :], -jnp.inf))
    inter = lanes(p["inter_t"][:, h:h + 1])
    floor = lanes(p["floor_t"][:, h:h + 1])
    yield
    s = (_dot(qb, kt.astype(BF16)) * pm).astype(BF16)
    qc = _dot(qb, caug.astype(BF16))
    yield
    sv = _dot(s, vaug)
    ap = p["a_prev"][:, h:h + 1]
    al = p["a_loc"][:, h:h + 1]
    c_new = ap * caug + al * _dot((kt * p["e_r"][h:h + 1, :]).astype(BF16), vaug)
    yield
    blocks = [sv[:, c:c + LANES] + inter * qc[:, c:c + LANES] for c in range(0, dv + LANES, LANES)]
    rden = 1.0 / jnp.maximum(jnp.abs(blocks[-1]), floor)
    h_out = jnp.concatenate([b * rden for b in blocks[:-1]], axis=1)
    return h_out, c_new


def _mlstm_kernel(qkf_ref, vf_ref, gtf_ref, gtf_next_ref, qkb_ref, vb_ref, gtb_ref, gtb_next_ref,
                  gb_ref, hf_ref, hb_ref, cf_ref, tabf_ref, rowf_ref, cb_ref, tabb_ref, rowb_ref):
    def park(tab_ref, row_ref, tables):
        for k, name in enumerate(MLSTM_TABLES):
            tab_ref[k] = tables[name]
        for k, name in enumerate(MLSTM_ROWS):
            row_ref[k] = tables[name]

    @pl.when(pl.program_id(1) == 0)
    def _():
        cf_ref[...] = jnp.zeros_like(cf_ref)
        cb_ref[...] = jnp.zeros_like(cb_ref)
        zero = jnp.zeros((1, LANES), F32)
        first = _lockstep([_mlstm_prep(_sub_rows(gtf_ref, 0, False), gb_ref, zero, reverse=False),
                           _mlstm_prep(_sub_rows(gtb_ref, 0, True), gb_ref, zero, reverse=True)])
        park(tabf_ref, rowf_ref, first[0])
        park(tabb_ref, rowb_ref, first[1])

    for s in range(SCAN_SUBSTEPS):
        last = s == SCAN_SUBSTEPS - 1
        _mlstm_substep(
            _sub_rows(qkf_ref, s, False), _sub_rows(vf_ref, s, False),
            _sub_rows(gtf_next_ref, 0, False) if last else _sub_rows(gtf_ref, s + 1, False),
            _sub_rows(qkb_ref, s, True), _sub_rows(vb_ref, s, True),
            _sub_rows(gtb_next_ref, 0, True) if last else _sub_rows(gtb_ref, s + 1, True),
            gb_ref, _sub_rows(hf_ref, s, False), _sub_rows(hb_ref, s, True),
            cf_ref, tabf_ref, rowf_ref, cb_ref, tabb_ref, rowb_ref, park)


def _mlstm_substep(qkf_ref, vf_ref, gtf_next_ref, qkb_ref, vb_ref, gtb_next_ref, gb_ref, hf_ref, hb_ref,
                   cf_ref, tabf_ref, rowf_ref, cb_ref, tabb_ref, rowb_ref, park):
    dv = vf_ref.shape[-1] // MLSTM_HEADS
    chains = []
    for (tab_ref, row_ref, qk_ref, v_ref, c_ref, h_ref, rev) in (
            (tabf_ref, rowf_ref, qkf_ref, vf_ref, cf_ref, hf_ref, False),
            (tabb_ref, rowb_ref, qkb_ref, vb_ref, cb_ref, hb_ref, True)):
        p = {name: tab_ref[k] for k, name in enumerate(MLSTM_TABLES)}
        p.update({name: row_ref[k] for k, name in enumerate(MLSTM_ROWS)})
        p["keep"] = _mlstm_keep(rev)
        chains.append((p, qk_ref, v_ref, c_ref, h_ref))
    units = [(p, qk_ref, v_ref, c_ref, h_ref, h)
             for (p, qk_ref, v_ref, c_ref, h_ref) in chains for h in range(MLSTM_HEADS)]
    results = _lockstep(
        [_mlstm_head(p, qk_ref, v_ref, c_ref, h) for (p, qk_ref, v_ref, c_ref, _, h) in units]
        + [_mlstm_prep(gtf_next_ref, gb_ref, chains[0][0]["m_new"], reverse=False),
           _mlstm_prep(gtb_next_ref, gb_ref, chains[1][0]["m_new"], reverse=True)])
    for (_, _, _, c_ref, h_ref, h), (h_out, c_new) in zip(units, results):
        h_ref[:, h * dv:(h + 1) * dv] = h_out.astype(h_ref.dtype)
        c_ref[h] = c_new
    park(tabf_ref, rowf_ref, results[-2])
    park(tabb_ref, rowb_ref, results[-1])


def _mlstm_scan(lay, qk, v, gates, gate_b):
    qkw, vw = qk.shape[1], v.shape[1]
    nh = MLSTM_HEADS
    dk, dv = qkw // (2 * nh), vw // nh
    fwd, bwd = _chunk_block(lay, False), _chunk_block(lay, True)
    ncc, nlc = lay.ctx_chunks // SCAN_SUBSTEPS, lay.lat_chunks // SCAN_SUBSTEPS
    out_f = lambda b, i: b * nlc + jnp.maximum(i - ncc, 0)
    out_b = lambda b, i: b * nlc + nlc - 1 - jnp.maximum(i - ncc, 0)
    assert CHUNK == LANES
    n_steps = ncc + nlc
    after = lambda order: (lambda b, i: order(b, jnp.minimum(i + 1, n_steps - 1)))
    blk = lambda width, order: pl.BlockSpec((SCAN_SUBSTEPS * CHUNK, width),
                                            lambda b, i: (order(b, i), 0))
    ins = lambda order: [blk(qkw, order), blk(vw, order), blk(LANES, order), blk(LANES, after(order))]
    state = [pltpu.VMEM((nh, dk, dv + LANES), F32),
             pltpu.VMEM((len(MLSTM_TABLES), CHUNK, LANES), F32),
             pltpu.VMEM((len(MLSTM_ROWS), 1, LANES), F32)]
    return pl.pallas_call(
        _mlstm_kernel,
        grid=(lay.bsz, n_steps),
        in_specs=ins(fwd) + ins(bwd) + [pl.BlockSpec((1, LANES), lambda b, i: (0, 0))],
        out_specs=[blk(vw, out_f), blk(vw, out_b)],
        out_shape=[jax.ShapeDtypeStruct((lay.lat_rows, vw), BF16)] * 2,
        scratch_shapes=state + state,
        compiler_params=_params(2),
        name="mlstm_scan",
    )(qk, v, gates, gates, qk, v, gates, gates, gate_b)


def _mlstm_out_up_kernel(hf_ref, hb_ref, o_ref_in, x_ref, g_ref, nw_ref, w_ref,
                         n2_ref, sh_ref, sc_ref, wup_ref, x_out_ref, a_ref, gl_ref):
    nh = MLSTM_HEADS
    dv = hf_ref.shape[-1] // nh
    for r0 in range(0, hf_ref.shape[0], OUT_ROW_BLOCK):
        rows = slice(r0, r0 + OUT_ROW_BLOCK)
        parts = []
        for h in range(nh):
            hh = (hf_ref[rows, h * dv:(h + 1) * dv].astype(F32)
                  + hb_ref[rows, h * dv:(h + 1) * dv].astype(F32))
            parts.append(_rms_scale(hh))
        hn = jnp.concatenate(parts, axis=1)
        y = (hn * nw_ref[...] * jax.nn.sigmoid(o_ref_in[rows, :].astype(F32))).astype(BF16)
        x_out_ref[rows, :] = x_ref[rows, :] + g_ref[...] * _dot(y, w_ref[...])
    _up_project(x_out_ref[...], n2_ref, sh_ref, sc_ref, wup_ref, a_ref, gl_ref)


def _mlstm_out_up(lay, hf, hb, o_gate, xall, mods, layer, norm_w, w_out, norm2_w, w_up):
    vw = hf.shape[1]
    d = xall.shape[1]
    f = w_up.shape[2] // 2
    tm = lay.tm
    lat = lambda c: _rows_spec(tm, c, lambda j: j)
    allt = lambda c: _rows_spec(tm, c, lambda j: j + lay.ctx_tiles)
    row_fn = lambda j: j // lay.lat_tiles_per_batch
    return pl.pallas_call(
        _mlstm_out_up_kernel, grid=(lay.lat_tiles,),
        in_specs=[lat(vw), lat(vw), allt(vw), allt(d),
                  _mod_spec(d, layer, 2, row_fn), _full(norm_w), _layer_spec(w_out, 0),
                  _full(norm2_w), _mod_spec(d, layer, 3, row_fn), _mod_spec(d, layer, 4, row_fn),
                  _layer_spec(w_up, layer)],
        out_specs=[lat(d), lat(f), lat(f)],
        out_shape=[jax.ShapeDtypeStruct((lay.lat_rows, d), F32),
                   jax.ShapeDtypeStruct((lay.lat_rows, f), F32),
                   jax.ShapeDtypeStruct((lay.lat_rows, f), BF16)],
        compiler_params=_params(), name="mlstm_out_ffn_up",
    )(hf, hb, o_gate, xall, mods, norm_w, w_out, norm2_w, mods, mods, w_up)


def _ffn_down_kernel(*refs, lay, skip, final_norm):
    if final_norm:
        a_ref, ap_ref, an_ref, gl_ref, x_ref, g_ref, cw_ref, cb_ref, w_ref, fw_ref, o_ref = refs
    else:
        a_ref, ap_ref, an_ref, gl_ref, x_ref, g_ref, cw_ref, cb_ref, w_ref, o_ref = refs
    tm, f = a_ref.shape
    gw = GRID_WIDTH
    j = pl.program_id(0) + skip
    is_ctx = j < lay.ctx_tiles
    pos = lay.batch_pos(j)
    has_prev = jnp.logical_and(jnp.logical_not(is_ctx), pos > 0)
    has_next = jnp.logical_and(jnp.logical_not(is_ctx), pos < lay.lat_tiles_per_batch - 1)

    cw = FFN_COL_CHUNK
    vert = jnp.where(is_ctx, 0.0, 1.0)
    sb = FFN_SHIFT_BLOCK
    ri = lax.broadcasted_iota(jnp.int32, (sb, sb), 0)
    ci = lax.broadcasted_iota(jnp.int32, (sb, sb), 1)
    colpos = jnp.where(is_ctx, ri % lay.n_ctx, ri & (gw - 1))
    last_col = jnp.where(is_ctx, lay.n_ctx - 1, gw - 1)
    take_left = jnp.where(colpos != 0, jnp.where(ci == ri - 1, 1.0, 0.0), 0.0)
    take_right = jnp.where(colpos != last_col, jnp.where(ci == ri + 1, 1.0, 0.0), 0.0)
    shift = jnp.concatenate([take_left, take_right], axis=1).astype(BF16)

    def taps(c0):
        cs = slice(c0, c0 + cw)
        up = jnp.where(has_prev, ap_ref[:, cs], 0.0)
        dn = jnp.where(has_next, an_ref[:, cs], 0.0)
        slabs = [jnp.concatenate([up, a_ref[0:tm - gw, cs]], axis=0), a_ref[:, cs],
                 jnp.concatenate([a_ref[gw:tm, cs], dn], axis=0)]

        def vsum(dj, parts, dtype):
            k = lambda di: 3 * (di + 1) + (dj + 1)
            return (parts[0] * (cw_ref[k(-1):k(-1) + 1, cs] * vert).astype(dtype)
                    + parts[1] * cw_ref[k(0):k(0) + 1, cs].astype(dtype)
                    + parts[2] * (cw_ref[k(1):k(1) + 1, cs] * vert).astype(dtype))

        slabs_b = [s.astype(BF16) for s in slabs]
        left, right = vsum(-1, slabs_b, BF16), vsum(1, slabs_b, BF16)
        sides = jnp.concatenate(
            [_dot(shift, jnp.concatenate([left[r0:r0 + sb], right[r0:r0 + sb]], axis=0))
             for r0 in range(0, tm, sb)], axis=0)
        return vsum(0, slabs, F32) + cb_ref[:, cs], sides

    def finish(c0, centre, sides, acc_out):
        cs = slice(c0, c0 + cw)
        mid = (_silu(centre + sides) * gl_ref[:, cs].astype(F32)).astype(BF16)
        part = _dot(mid, w_ref[cs, :])
        return part if acc_out is None else acc_out + part

    acc_out = None
    pending = None
    for c0 in range(0, f, cw):
        cur = (c0,) + taps(c0)
        if pending is not None:
            acc_out = finish(*pending, acc_out)
        pending = cur
    acc_out = finish(*pending, acc_out)
    out = x_ref[...] + g_ref[...] * acc_out
    if final_norm:
        out = _rms_scale(out) * fw_ref[...]
    o_ref[...] = out


def _ffn_down(lay, a, gl, xres, mods, layer, conv_w9, conv_b, w_down, *, skip=0, final_w=None):
    tt, f = a.shape
    d = xres.shape[1]
    tm = lay.tm
    gw = GRID_WIDTH
    assert f % FFN_COL_CHUNK == 0 and tm % FFN_SHIFT_BLOCK == 0
    assert FFN_SHIFT_BLOCK % gw == 0 and FFN_SHIFT_BLOCK % lay.n_ctx == 0
    n_tiles = tt // tm
    hb = tm // gw
    tok = lambda c: _rows_spec(tm, c, lambda j: j)
    in_specs = [
        tok(f),
        _rows_spec(gw, f, lambda j: jnp.maximum(j * hb - 1, 0)),
        _rows_spec(gw, f, lambda j: jnp.minimum((j + 1) * hb, tt // gw - 1)),
        tok(f), tok(d), _mod_spec(d, layer, 5, lambda j: lay.mod_row(j + skip)),
        _full(conv_w9), _full(conv_b), _layer_spec(w_down, layer)]
    args = [a, a, a, gl, xres, mods, conv_w9, conv_b, w_down]
    if final_w is not None:
        in_specs.append(_full(final_w))
        args.append(final_w)
    kern = functools.partial(_ffn_down_kernel, lay=lay, skip=skip, final_norm=final_w is not None)
    return pl.pallas_call(
        kern, grid=(n_tiles,), in_specs=in_specs, out_specs=tok(d),
        out_shape=jax.ShapeDtypeStruct((tt, d), F32),
        compiler_params=_params(), name=f"ffn_down_l{layer}",
    )(*args)


def _pad_cols(a, n):
    return jnp.pad(a, [(0, 0)] * (a.ndim - 1) + [(0, n - a.shape[-1])])


def kernel(x, c, ctx, c_ctx, ada_w, ada_b, norm1_w, norm2_w, ssm_w_in, ssm_conv_w, ssm_conv_b, ssm_dt_bias, ssm_a_log, ssm_d, ssm_norm_w, ssm_w_out, mlstm_w_in, mlstm_conv_w, mlstm_conv_b, mlstm_gate_b, mlstm_norm_w, mlstm_w_out, ffn_w_up, ffn_conv_w, ffn_conv_b, ffn_w_down, final_norm_w):
    bsz, seq, d = x.shape
    n_ctx = ctx.shape[1]
    depth = ada_w.shape[0]
    assert depth == 2 and ssm_w_in.shape[0] == 1 and mlstm_w_in.shape[0] == 1
    lay = _Layout(bsz, n_ctx, seq)
    x_flat = x.reshape(bsz * seq, d)
    ctx_flat = ctx.reshape(bsz * n_ctx, d)

    n_rows = -(-(bsz + 1) // SUBLANES) * SUBLANES
    cvec = jnp.zeros((n_rows, d), F32).at[:bsz].set(c).at[bsz].set(c_ctx)
    mods = _ada(cvec, ada_w, ada_b).reshape(depth, n_rows, 6, 1, d)

    w_up = ffn_w_up.astype(BF16)
    w_down = ffn_w_down.astype(BF16)
    f = ffn_conv_b.shape[1]
    conv9 = ffn_conv_w.reshape(depth, 9, f)

    heads = ssm_d.shape[1]
    assert 2 * heads <= LANES
    inner = heads * SSM_HEAD_DIM
    bc_w = 2 * SSM_GROUPS * SSM_STATE
    conv_ch = inner + bc_w
    w_in = _pad_cols(ssm_w_in, 2 * inner + bc_w + LANES).astype(BF16)
    splits = [(0, inner), (inner, conv_ch), (inner + conv_ch, LANES)]
    z, xbc, dt = _project(
        lay, (ctx_flat, x_flat), norm1_w[0:1], mods, 0, 0, 1, w_in, splits, (BF16, BF16, F32),
        conv=(1, ssm_conv_w[0], ssm_conv_b[0:1]), name="in_proj_l0")
    dtb = _pad_cols(ssm_dt_bias[0].reshape(1, -1), LANES)
    alog = _pad_cols(ssm_a_log[0].reshape(1, -1), LANES)
    y_f, y_b = _ssd_scan(lay, xbc, dt, dtb, alog, ssm_d[0], heads=heads)
    xall, a, gl = _ssm_out_up(lay, y_f, y_b, z, (ctx_flat, x_flat), mods, 0, ssm_norm_w[0:1],
                              ssm_w_out.astype(BF16), norm2_w[0:1], w_up)
    xall = _ffn_down(lay, a, gl, xall, mods, 0, conv9[0], ffn_conv_b[0:1], w_down)

    qkw = mlstm_conv_b.shape[1]
    vw = mlstm_norm_w.shape[1]
    w_in = _pad_cols(mlstm_w_in, qkw + 2 * vw + LANES).astype(BF16)
    splits = [(0, qkw), (qkw, vw), (qkw + vw, vw), (qkw + 2 * vw, LANES)]
    qk, v, o_gate, gates = _project(
        lay, (xall,), norm1_w[1:2], mods, 1, 0, 1, w_in, splits, (BF16, BF16, BF16, F32),
        conv=(0, mlstm_conv_w[0], mlstm_conv_b[0:1]), name="in_proj_l1")
    gate_b = _pad_cols(mlstm_gate_b[0:1], LANES)
    h_f, h_b = _mlstm_scan(lay, qk, v, gates, gate_b)
    xlat, a, gl = _mlstm_out_up(lay, h_f, h_b, o_gate, xall, mods, 1, mlstm_norm_w[0:1],
                                mlstm_w_out.astype(BF16), norm2_w[1:2], w_up)
    out = _ffn_down(lay, a, gl, xlat, mods, 1, conv9[1], ffn_conv_b[1:2], w_down,
                    skip=lay.ctx_tiles, final_w=final_norm_w.reshape(1, d))
    return out.reshape(bsz, seq, d)
```
